```python
import jax, jax.numpy as jnp
from jax import lax
import numpy as np

D_MODEL = 1024
BATCH = 8
SEQ = 2048
DEPTH = 4

CHUNK = 64
RMS_EPS = 1e-6
DN_HEADS = 4
DN_DK = 128
DN_DV = 128
DN_CONV = 4
DN_QKV = 2 * DN_HEADS * DN_DK + DN_HEADS * DN_DV
DN_VAL = DN_HEADS * DN_DV
RET_HEADS = 4
RET_DK = 128
RET_DV = 128
RET_QK = RET_HEADS * RET_DK
RET_VAL = RET_HEADS * RET_DV
ROPE_BASE = 10000.0
GN_EPS = 1e-5
POOL_WINDOWS = (2, 4, 8, 16)
POOL_GROUPS = 4
POOL_GROUP = 128
POOL_WIDTH = POOL_GROUPS * POOL_GROUP
N_BRANCH = 3
D_FF = -(-8 * D_MODEL // (3 * 256)) * 256
IN_SIZES = (DN_QKV, DN_VAL, DN_HEADS, DN_HEADS, RET_QK, RET_QK, RET_VAL, RET_VAL, POOL_WIDTH, N_BRANCH * D_MODEL)
D_IN = sum(IN_SIZES)

kernel_name = "hybrid_deltanet_pool_retention_block"


def _split_points(sizes):
    pts, acc = [], 0
    for s in sizes[:-1]:
        acc += s
        pts.append(acc)
    return pts


def rms_norm(x, w):
    xf = x.astype(jnp.float32)
    y = xf * lax.rsqrt(jnp.mean(xf * xf, axis=-1, keepdims=True) + RMS_EPS) * w.astype(jnp.float32)
    return y.astype(x.dtype)


def l2_normalize(x):
    return x * lax.rsqrt(jnp.sum(x * x, axis=-1, keepdims=True) + 1e-6)


def causal_depthwise_conv(x, w):
    k = w.shape[0]
    return lax.conv_general_dilated(
        x, w[:, None, :].astype(x.dtype), window_strides=(1,), padding=((k - 1, 0),),
        dimension_numbers=("NWC", "WIO", "NWC"), feature_group_count=x.shape[-1])


def to_chunks(x):
    b, s, h, d = x.shape
    return x.reshape(b, s // CHUNK, CHUNK, h, d).transpose(0, 3, 1, 2, 4)


def from_chunks(x):
    b, h, n, c, d = x.shape
    return x.transpose(0, 2, 3, 1, 4).reshape(b, n * c, h * d)


def gated_delta_rule(q, k, v, beta, g):
    b, s, h, dk = q.shape
    dv = v.shape[-1]
    n = s // CHUNK
    q = to_chunks(q) * (dk ** -0.5)
    k = to_chunks(k)
    v = to_chunks(v)
    beta = beta.reshape(b, n, CHUNK, h).transpose(0, 3, 1, 2)
    g = g.reshape(b, n, CHUNK, h).transpose(0, 3, 1, 2)
    G = jnp.cumsum(g, axis=-1)
    causal = jnp.tril(jnp.ones((CHUNK, CHUNK), dtype=bool))
    strict = jnp.tril(jnp.ones((CHUNK, CHUNK), dtype=bool), k=-1)
    diff = G[..., :, None] - G[..., None, :]
    decay = jnp.where(causal, jnp.exp(jnp.where(causal, diff, 0.0)), 0.0)
    k_beta = k * beta[..., None]
    m = jnp.where(strict, jnp.einsum('bhnid,bhnjd->bhnij', k_beta, k) * decay, 0.0)
    a = m + jnp.eye(CHUNK, dtype=m.dtype)
    rhs = jnp.concatenate([v * beta[..., None], k_beta * jnp.exp(G)[..., None]], axis=-1)
    sol = lax.linalg.triangular_solve(a, rhs, left_side=True, lower=True, unit_diagonal=True)
    u, w = sol[..., :dv], sol[..., dv:]
    attn = jnp.einsum('bhnid,bhnjd->bhnij', q, k) * decay
    g_last = G[..., -1]
    q_dec = q * jnp.exp(G)[..., None]
    k_tail = k * jnp.exp(g_last[..., None] - G)[..., None]
    xs = tuple(jnp.moveaxis(t, 2, 0) for t in (u, w, attn, q_dec, k_tail, jnp.exp(g_last)))

    def step(state, inp):
        u_c, w_c, a_c, qd_c, kt_c, dl_c = inp
        v_new = u_c - jnp.einsum('bhck,bhkv->bhcv', w_c, state)
        o = jnp.einsum('bhck,bhkv->bhcv', qd_c, state) + jnp.einsum('bhcj,bhjv->bhcv', a_c, v_new)
        state = state * dl_c[..., None, None] + jnp.einsum('bhck,bhcv->bhkv', kt_c, v_new)
        return state, o

    s0 = jnp.zeros((b, h, dk, dv), dtype=q.dtype)
    _, o = lax.scan(step, s0, xs)
    return from_chunks(jnp.moveaxis(o, 0, 2))


def rotary(x):
    s, d = x.shape[1], x.shape[-1]
    half = d // 2
    inv = ROPE_BASE ** (-jnp.arange(half, dtype=jnp.float32) / half)
    ang = jnp.arange(s, dtype=jnp.float32)[:, None] * inv[None, :]
    cos = jnp.cos(ang)[None, :, None, :]
    sin = jnp.sin(ang)[None, :, None, :]
    x1, x2 = x[..., :half], x[..., half:]
    return jnp.concatenate([x1 * cos - x2 * sin, x2 * cos + x1 * sin], axis=-1)


def chunkwise_retention(q, k, v, log_gamma):
    b, s, h, dk = q.shape
    dv = v.shape[-1]
    q = to_chunks(q)
    k = to_chunks(k) * (dk ** -0.5)
    v = to_chunks(v)
    idx = jnp.arange(CHUNK, dtype=jnp.float32)
    lg = log_gamma[:, None]
    dmask = jnp.exp(jnp.abs(idx[:, None] - idx[None, :])[None] * log_gamma[:, None, None])
    scores = jnp.einsum('bhnid,bhnjd->bhnij', q, k) * dmask[:, None]
    o_inner = jnp.einsum('bhnij,bhnjv->bhniv', scores, v)
    xi = jnp.exp((idx + 1.0)[None] * lg)
    zeta = jnp.exp((CHUNK - 1.0 - idx)[None] * lg)
    kv = jnp.einsum('bhnjd,bhnjv->bhndv', k * zeta[:, None, :, None], v)
    chunk_decay = jnp.exp(CHUNK * log_gamma)

    def step(r, kv_c):
        return r * chunk_decay[:, None, None] + kv_c, r

    _, r_prev = lax.scan(step, jnp.zeros((b, h, dk, dv), dtype=kv.dtype), jnp.moveaxis(kv, 2, 0))
    r_prev = jnp.moveaxis(r_prev, 0, 2)
    o_cross = jnp.einsum('bhnid,bhndv->bhniv', q * xi[:, None, :, None], r_prev)
    o = from_chunks(o_inner + o_cross)
    return o.reshape(b, s, h, dv)


def multiscale_pool(u, w_lin, scale):
    b, s, _ = u.shape
    groups = u.reshape(b, s, POOL_GROUPS, POOL_GROUP)
    cs = jnp.cumsum(groups, axis=1)
    t = jnp.arange(1, s + 1, dtype=jnp.float32)
    pooled = []
    for gi, win in enumerate(POOL_WINDOWS):
        c = cs[:, :, gi]
        prev = jnp.pad(c, ((0, 0), (win, 0), (0, 0)))[:, :s]
        cnt = jnp.minimum(t, float(win))[None, :, None]
        pooled.append((c - prev) / cnt)
    mixed = jnp.stack(pooled, axis=2) - groups
    y = jnp.einsum('bsgc,gcd->bsgd', mixed, w_lin.astype(jnp.float32)).reshape(b, s, POOL_WIDTH)
    return y * scale.astype(jnp.float32)


def hybrid_mixer(h, w_in, dn_conv, dn_A_log, dn_dt_bias, dn_out_norm, ret_out_norm,
                 pool_w, pool_scale, w_branch_dn, w_branch_ret, w_branch_pool, w_out):
    b, s, _ = h.shape
    f32 = jnp.float32
    proj = h @ w_in
    (dn_qkv, dn_z, dn_b, dn_a, ret_q, ret_k, ret_v, ret_g, pool_u, gates) = jnp.split(
        proj, _split_points(IN_SIZES), axis=-1)

    qkv = jax.nn.silu(causal_depthwise_conv(dn_qkv, dn_conv)).astype(f32)
    q, k, v = jnp.split(qkv, [DN_HEADS * DN_DK, 2 * DN_HEADS * DN_DK], axis=-1)
    q = l2_normalize(q.reshape(b, s, DN_HEADS, DN_DK))
    k = l2_normalize(k.reshape(b, s, DN_HEADS, DN_DK))
    v = v.reshape(b, s, DN_HEADS, DN_DV)
    beta = jax.nn.sigmoid(dn_b.astype(f32))
    g = -jnp.exp(dn_A_log.astype(f32)) * jax.nn.softplus(dn_a.astype(f32) + dn_dt_bias.astype(f32))
    o_dn = gated_delta_rule(q, k, v, beta, g).reshape(b, s, DN_HEADS, DN_DV)
    o_dn = o_dn * lax.rsqrt(jnp.mean(o_dn * o_dn, axis=-1, keepdims=True) + RMS_EPS) * dn_out_norm.astype(f32)
    o_dn = o_dn.reshape(b, s, DN_VAL) * jax.nn.silu(dn_z.astype(f32))

    o_pool = multiscale_pool(pool_u.astype(f32), pool_w, pool_scale)

    log_gamma = jnp.log1p(-jnp.exp2(-5.0 - jnp.arange(RET_HEADS, dtype=f32)))
    rq = rotary(ret_q.astype(f32).reshape(b, s, RET_HEADS, RET_DK))
    rk = rotary(ret_k.astype(f32).reshape(b, s, RET_HEADS, RET_DK))
    rv = ret_v.astype(f32).reshape(b, s, RET_HEADS, RET_DV)
    o_ret = chunkwise_retention(rq, rk, rv, log_gamma)
    mu = jnp.mean(o_ret, axis=-1, keepdims=True)
    var = jnp.mean(jnp.square(o_ret - mu), axis=-1, keepdims=True)
    o_ret = ((o_ret - mu) * lax.rsqrt(var + GN_EPS)).reshape(b, s, RET_VAL) * ret_out_norm.astype(f32)
    o_ret = o_ret * jax.nn.silu(ret_g.astype(f32))

    gt = jax.nn.sigmoid(gates.astype(f32)).reshape(b, s, N_BRANCH, D_MODEL)
    y = (gt[:, :, 0] * (o_dn.astype(h.dtype) @ w_branch_dn)
         + gt[:, :, 1] * (o_pool.astype(h.dtype) @ w_branch_pool)
         + gt[:, :, 2] * (o_ret.astype(h.dtype) @ w_branch_ret))
    return y.astype(h.dtype) @ w_out


def swiglu(h, w_gate, w_up, w_down):
    return (jax.nn.silu(h @ w_gate) * (h @ w_up)) @ w_down


def _fwd_setup_inputs(seed: int = 0) -> dict:
    key = jax.random.key(seed)
    ks = jax.random.split(key, 24)
    f32 = jnp.float32
    L = DEPTH

    def nrm(k, shape, fan_in):
        return jax.random.normal(k, shape, f32) * (fan_in ** -0.5)

    def gain(k, shape):
        return 1.0 + 0.1 * jax.random.normal(k, shape, f32)

    dt = jnp.exp(jax.random.uniform(ks[6], (L, DN_HEADS), f32, minval=float(np.log(1e-3)), maxval=float(np.log(1e-1))))
    return {
        "x": jax.random.normal(ks[0], (BATCH, SEQ, D_MODEL), f32),
        "mix_pre_norm": gain(ks[1], (L, D_MODEL)),
        "mix_post_norm": gain(ks[2], (L, D_MODEL)),
        "w_in": nrm(ks[3], (L, D_MODEL, D_IN), D_MODEL),
        "dn_conv": nrm(ks[4], (L, DN_CONV, DN_QKV), DN_CONV),
        "dn_A_log": jnp.log(jax.random.uniform(ks[5], (L, DN_HEADS), f32, minval=1.0, maxval=16.0)),
        "dn_dt_bias": dt + jnp.log(-jnp.expm1(-dt)),
        "dn_out_norm": gain(ks[7], (L, DN_DV)),
        "ret_out_norm": gain(ks[8], (L, RET_VAL)),
        "pool_w": nrm(ks[9], (L, POOL_GROUPS, POOL_GROUP, POOL_GROUP), POOL_GROUP),
        "pool_scale": gain(ks[10], (L, POOL_WIDTH)),
        "w_branch_dn": nrm(ks[11], (L, DN_VAL, D_MODEL), DN_VAL),
        "w_branch_ret": nrm(ks[12], (L, RET_VAL, D_MODEL), RET_VAL),
        "w_branch_pool": nrm(ks[13], (L, POOL_WIDTH, D_MODEL), POOL_WIDTH),
        "w_out": nrm(ks[14], (L, D_MODEL, D_MODEL), D_MODEL),
        "ffn_pre_norm": gain(ks[15], (L, D_MODEL)),
        "ffn_post_norm": gain(ks[16], (L, D_MODEL)),
        "ffn_gate": nrm(ks[17], (L, D_MODEL, D_FF), D_MODEL),
        "ffn_up": nrm(ks[18], (L, D_MODEL, D_FF), D_MODEL),
        "ffn_down": nrm(ks[19], (L, D_FF, D_MODEL), D_FF),
    }


def _fwd_reference(x, mix_pre_norm, mix_post_norm, w_in, dn_conv, dn_A_log, dn_dt_bias, dn_out_norm,
              ret_out_norm, pool_w, pool_scale, w_branch_dn, w_branch_ret, w_branch_pool, w_out,
              ffn_pre_norm, ffn_post_norm, ffn_gate, ffn_up, ffn_down):
    for l in range(DEPTH):
        h = rms_norm(x, mix_pre_norm[l])
        m = hybrid_mixer(h, w_in[l], dn_conv[l], dn_A_log[l], dn_dt_bias[l], dn_out_norm[l],
                         ret_out_norm[l], pool_w[l], pool_scale[l], w_branch_dn[l], w_branch_ret[l],
                         w_branch_pool[l], w_out[l])
        x = x + rms_norm(m, mix_post_norm[l])
        h = rms_norm(x, ffn_pre_norm[l])
        x = x + rms_norm(swiglu(h, ffn_gate[l], ffn_up[l], ffn_down[l]), ffn_post_norm[l])
    return x


import jax as _jax
import jax.numpy as _jnp

TWIN_FORMAT = 'train_step'
FWD_PARAMS = ['x', 'mix_pre_norm', 'mix_post_norm', 'w_in', 'dn_conv', 'dn_A_log', 'dn_dt_bias', 'dn_out_norm', 'ret_out_norm', 'pool_w', 'pool_scale', 'w_branch_dn', 'w_branch_ret', 'w_branch_pool', 'w_out', 'ffn_pre_norm', 'ffn_post_norm', 'ffn_gate', 'ffn_up', 'ffn_down']
TWIN_WEIGHTS = ['mix_pre_norm', 'mix_post_norm', 'w_in', 'dn_conv', 'dn_A_log', 'dn_dt_bias', 'dn_out_norm', 'ret_out_norm', 'pool_w', 'pool_scale', 'w_branch_dn', 'w_branch_ret', 'w_branch_pool', 'w_out', 'ffn_pre_norm', 'ffn_post_norm', 'ffn_gate', 'ffn_up', 'ffn_down']
TWIN_DIFF_INPUT = 'x'
TWIN_INPUTS = ['x', 'mix_pre_norm', 'mix_post_norm', 'w_in', 'dn_conv', 'dn_A_log', 'dn_dt_bias', 'dn_out_norm', 'ret_out_norm', 'pool_w', 'pool_scale', 'w_branch_dn', 'w_branch_ret', 'w_branch_pool', 'w_out', 'ffn_pre_norm', 'ffn_post_norm', 'ffn_gate', 'ffn_up', 'ffn_down', 'loss_target', 'm_mix_pre_norm', 'm_mix_post_norm', 'm_w_in', 'm_dn_conv', 'm_dn_A_log', 'm_dn_dt_bias', 'm_dn_out_norm', 'm_ret_out_norm', 'm_pool_w', 'm_pool_scale', 'm_w_branch_dn', 'm_w_branch_ret', 'm_w_branch_pool', 'm_w_out', 'm_ffn_pre_norm', 'm_ffn_post_norm', 'm_ffn_gate', 'm_ffn_up', 'm_ffn_down', 'v_mix_pre_norm', 'v_mix_post_norm', 'v_w_in', 'v_dn_conv', 'v_dn_A_log', 'v_dn_dt_bias', 'v_dn_out_norm', 'v_ret_out_norm', 'v_pool_w', 'v_pool_scale', 'v_w_branch_dn', 'v_w_branch_ret', 'v_w_branch_pool', 'v_w_out', 'v_ffn_pre_norm', 'v_ffn_post_norm', 'v_ffn_gate', 'v_ffn_up', 'v_ffn_down']
TWIN_OUTPUTS = ['loss', 'grad_x', 'grad_mix_pre_norm', 'grad_mix_post_norm', 'grad_w_in', 'grad_dn_conv', 'grad_dn_A_log', 'grad_dn_dt_bias', 'grad_dn_out_norm', 'grad_ret_out_norm', 'grad_pool_w', 'grad_pool_scale', 'grad_w_branch_dn', 'grad_w_branch_ret', 'grad_w_branch_pool', 'grad_w_out', 'grad_ffn_pre_norm', 'grad_ffn_post_norm', 'grad_ffn_gate', 'grad_ffn_up', 'grad_ffn_down', 'delta_mix_pre_norm', 'delta_mix_post_norm', 'delta_w_in', 'delta_dn_conv', 'delta_dn_A_log', 'delta_dn_dt_bias', 'delta_dn_out_norm', 'delta_ret_out_norm', 'delta_pool_w', 'delta_pool_scale', 'delta_w_branch_dn', 'delta_w_branch_ret', 'delta_w_branch_pool', 'delta_w_out', 'delta_ffn_pre_norm', 'delta_ffn_post_norm', 'delta_ffn_gate', 'delta_ffn_up', 'delta_ffn_down', 'new_m_mix_pre_norm', 'new_m_mix_post_norm', 'new_m_w_in', 'new_m_dn_conv', 'new_m_dn_A_log', 'new_m_dn_dt_bias', 'new_m_dn_out_norm', 'new_m_ret_out_norm', 'new_m_pool_w', 'new_m_pool_scale', 'new_m_w_branch_dn', 'new_m_w_branch_ret', 'new_m_w_branch_pool', 'new_m_w_out', 'new_m_ffn_pre_norm', 'new_m_ffn_post_norm', 'new_m_ffn_gate', 'new_m_ffn_up', 'new_m_ffn_down', 'new_v_mix_pre_norm', 'new_v_mix_post_norm', 'new_v_w_in', 'new_v_dn_conv', 'new_v_dn_A_log', 'new_v_dn_dt_bias', 'new_v_dn_out_norm', 'new_v_ret_out_norm', 'new_v_pool_w', 'new_v_pool_scale', 'new_v_w_branch_dn', 'new_v_w_branch_ret', 'new_v_w_branch_pool', 'new_v_w_out', 'new_v_ffn_pre_norm', 'new_v_ffn_post_norm', 'new_v_ffn_gate', 'new_v_ffn_up', 'new_v_ffn_down']
TWIN_LEAF_KINDS = {'loss': 'loss', 'grad_x': 'grad_x', 'grad_mix_pre_norm': 'grad_w', 'grad_mix_post_norm': 'grad_w', 'grad_w_in': 'grad_w', 'grad_dn_conv': 'grad_w', 'grad_dn_A_log': 'grad_w', 'grad_dn_dt_bias': 'grad_w', 'grad_dn_out_norm': 'grad_w', 'grad_ret_out_norm': 'grad_w', 'grad_pool_w': 'grad_w', 'grad_pool_scale': 'grad_w', 'grad_w_branch_dn': 'grad_w', 'grad_w_branch_ret': 'grad_w', 'grad_w_branch_pool': 'grad_w', 'grad_w_out': 'grad_w', 'grad_ffn_pre_norm': 'grad_w', 'grad_ffn_post_norm': 'grad_w', 'grad_ffn_gate': 'grad_w', 'grad_ffn_up': 'grad_w', 'grad_ffn_down': 'grad_w', 'delta_mix_pre_norm': 'delta_w', 'delta_mix_post_norm': 'delta_w', 'delta_w_in': 'delta_w', 'delta_dn_conv': 'delta_w', 'delta_dn_A_log': 'delta_w', 'delta_dn_dt_bias': 'delta_w', 'delta_dn_out_norm': 'delta_w', 'delta_ret_out_norm': 'delta_w', 'delta_pool_w': 'delta_w', 'delta_pool_scale': 'delta_w', 'delta_w_branch_dn': 'delta_w', 'delta_w_branch_ret': 'delta_w', 'delta_w_branch_pool': 'delta_w', 'delta_w_out': 'delta_w', 'delta_ffn_pre_norm': 'delta_w', 'delta_ffn_post_norm': 'delta_w', 'delta_ffn_gate': 'delta_w', 'delta_ffn_up': 'delta_w', 'delta_ffn_down': 'delta_w', 'new_m_mix_pre_norm': 'new_m', 'new_m_mix_post_norm': 'new_m', 'new_m_w_in': 'new_m', 'new_m_dn_conv': 'new_m', 'new_m_dn_A_log': 'new_m', 'new_m_dn_dt_bias': 'new_m', 'new_m_dn_out_norm': 'new_m', 'new_m_ret_out_norm': 'new_m', 'new_m_pool_w': 'new_m', 'new_m_pool_scale': 'new_m', 'new_m_w_branch_dn': 'new_m', 'new_m_w_branch_ret': 'new_m', 'new_m_w_branch_pool': 'new_m', 'new_m_w_out': 'new_m', 'new_m_ffn_pre_norm': 'new_m', 'new_m_ffn_post_norm': 'new_m', 'new_m_ffn_gate': 'new_m', 'new_m_ffn_up': 'new_m', 'new_m_ffn_down': 'new_m', 'new_v_mix_pre_norm': 'new_v', 'new_v_mix_post_norm': 'new_v', 'new_v_w_in': 'new_v', 'new_v_dn_conv': 'new_v', 'new_v_dn_A_log': 'new_v', 'new_v_dn_dt_bias': 'new_v', 'new_v_dn_out_norm': 'new_v', 'new_v_ret_out_norm': 'new_v', 'new_v_pool_w': 'new_v', 'new_v_pool_scale': 'new_v', 'new_v_w_branch_dn': 'new_v', 'new_v_w_branch_ret': 'new_v', 'new_v_w_branch_pool': 'new_v', 'new_v_w_out': 'new_v', 'new_v_ffn_pre_norm': 'new_v', 'new_v_ffn_post_norm': 'new_v', 'new_v_ffn_gate': 'new_v', 'new_v_ffn_up': 'new_v', 'new_v_ffn_down': 'new_v'}


def _forward(args):
    return _fwd_reference(*[args[k] for k in FWD_PARAMS])


def _output_shape():
    out = _jax.eval_shape(lambda: _forward(_fwd_setup_inputs(0)))
    return out.shape, out.dtype

N_MICROBATCH = 1
ADAM_LR = 0.001
ADAM_B1 = 0.9
ADAM_B2 = 0.999
ADAM_EPS = 1e-08
ADAM_WD = 0.01
ADAM_STEP = 10
PER_EXAMPLE_BATCH_AXIS = {'x': 0, 'loss_target': 0}
SHARED_INPUTS = []
_WEIGHT_DTYPES = {'mix_pre_norm': _jnp.float32, 'mix_post_norm': _jnp.float32, 'w_in': _jnp.float32, 'dn_conv': _jnp.float32, 'dn_A_log': _jnp.float32, 'dn_dt_bias': _jnp.float32, 'dn_out_norm': _jnp.float32, 'ret_out_norm': _jnp.float32, 'pool_w': _jnp.float32, 'pool_scale': _jnp.float32, 'w_branch_dn': _jnp.float32, 'w_branch_ret': _jnp.float32, 'w_branch_pool': _jnp.float32, 'w_out': _jnp.float32, 'ffn_pre_norm': _jnp.float32, 'ffn_post_norm': _jnp.float32, 'ffn_gate': _jnp.float32, 'ffn_up': _jnp.float32, 'ffn_down': _jnp.float32}
MOMENT_SCALE = {'mix_pre_norm': 1.710799e+00, 'mix_post_norm': 1.588446e+01, 'w_in': 5.912423e-01, 'dn_conv': 5.474737e-01, 'dn_A_log': 3.980687e+00, 'dn_dt_bias': 3.789004e+00, 'dn_out_norm': 1.782163e+00, 'ret_out_norm': 6.994363e-01, 'pool_w': 1.334661e+00, 'pool_scale': 1.306234e+00, 'w_branch_dn': 5.619168e-01, 'w_branch_ret': 4.996899e-01, 'w_branch_pool': 9.695277e-01, 'w_out': 1.243001e+00, 'ffn_pre_norm': 1.169775e+00, 'ffn_post_norm': 1.604365e+01, 'ffn_gate': 4.608519e-01, 'ffn_up': 5.013875e-01, 'ffn_down': 8.433636e-01}


def _to_microbatches(a, axis):
    t = _jnp.moveaxis(a, axis, 0)
    t = t.reshape((N_MICROBATCH, t.shape[0] // N_MICROBATCH) + t.shape[1:])
    return _jnp.moveaxis(t, 1, axis + 1)


def setup_inputs(seed: int = 0) -> dict:
    inp = _fwd_setup_inputs(seed)
    key = _jax.random.fold_in(_jax.random.key(seed), 7919)
    shape, _ = _output_shape()
    out = dict(inp)
    out["loss_target"] = _jax.random.normal(_jax.random.fold_in(key, 0), shape, _jnp.float32)
    for i, name in enumerate(TWIN_WEIGHTS):
        w = inp[name].astype(_jnp.float32)
        if MOMENT_SCALE is None:
            s = _jnp.sqrt(_jnp.mean(_jnp.square(w)) + 1e-30)
        else:
            s = MOMENT_SCALE[name]
        km, kv = _jax.random.split(_jax.random.fold_in(key, i + 1))
        out[name] = w
        out["m_" + name] = s * _jax.random.normal(km, w.shape, _jnp.float32)
        out["v_" + name] = (s * s) * _jax.random.uniform(kv, w.shape, _jnp.float32, 0.5, 1.5)
    if N_MICROBATCH > 1:
        for name, axis in PER_EXAMPLE_BATCH_AXIS.items():
            out[name] = _to_microbatches(out[name], axis)
    return {'x': out['x'], 'mix_pre_norm': out['mix_pre_norm'], 'mix_post_norm': out['mix_post_norm'], 'w_in': out['w_in'], 'dn_conv': out['dn_conv'], 'dn_A_log': out['dn_A_log'], 'dn_dt_bias': out['dn_dt_bias'], 'dn_out_norm': out['dn_out_norm'], 'ret_out_norm': out['ret_out_norm'], 'pool_w': out['pool_w'], 'pool_scale': out['pool_scale'], 'w_branch_dn': out['w_branch_dn'], 'w_branch_ret': out['w_branch_ret'], 'w_branch_pool': out['w_branch_pool'], 'w_out': out['w_out'], 'ffn_pre_norm': out['ffn_pre_norm'], 'ffn_post_norm': out['ffn_post_norm'], 'ffn_gate': out['ffn_gate'], 'ffn_up': out['ffn_up'], 'ffn_down': out['ffn_down'], 'loss_target': out['loss_target'], 'm_mix_pre_norm': out['m_mix_pre_norm'], 'm_mix_post_norm': out['m_mix_post_norm'], 'm_w_in': out['m_w_in'], 'm_dn_conv': out['m_dn_conv'], 'm_dn_A_log': out['m_dn_A_log'], 'm_dn_dt_bias': out['m_dn_dt_bias'], 'm_dn_out_norm': out['m_dn_out_norm'], 'm_ret_out_norm': out['m_ret_out_norm'], 'm_pool_w': out['m_pool_w'], 'm_pool_scale': out['m_pool_scale'], 'm_w_branch_dn': out['m_w_branch_dn'], 'm_w_branch_ret': out['m_w_branch_ret'], 'm_w_branch_pool': out['m_w_branch_pool'], 'm_w_out': out['m_w_out'], 'm_ffn_pre_norm': out['m_ffn_pre_norm'], 'm_ffn_post_norm': out['m_ffn_post_norm'], 'm_ffn_gate': out['m_ffn_gate'], 'm_ffn_up': out['m_ffn_up'], 'm_ffn_down': out['m_ffn_down'], 'v_mix_pre_norm': out['v_mix_pre_norm'], 'v_mix_post_norm': out['v_mix_post_norm'], 'v_w_in': out['v_w_in'], 'v_dn_conv': out['v_dn_conv'], 'v_dn_A_log': out['v_dn_A_log'], 'v_dn_dt_bias': out['v_dn_dt_bias'], 'v_dn_out_norm': out['v_dn_out_norm'], 'v_ret_out_norm': out['v_ret_out_norm'], 'v_pool_w': out['v_pool_w'], 'v_pool_scale': out['v_pool_scale'], 'v_w_branch_dn': out['v_w_branch_dn'], 'v_w_branch_ret': out['v_w_branch_ret'], 'v_w_branch_pool': out['v_w_branch_pool'], 'v_w_out': out['v_w_out'], 'v_ffn_pre_norm': out['v_ffn_pre_norm'], 'v_ffn_post_norm': out['v_ffn_post_norm'], 'v_ffn_gate': out['v_ffn_gate'], 'v_ffn_up': out['v_ffn_up'], 'v_ffn_down': out['v_ffn_down']}


def _loss(weights, diff, rest, loss_target):
    with _jax.named_scope("forward"):
        args = {**rest, TWIN_DIFF_INPUT: diff, **{k: w.astype(_WEIGHT_DTYPES[k]) for k, w in weights.items()}}
        y = _forward(args)
    with _jax.named_scope("loss_head"):
        err = _jnp.square(y.astype(_jnp.float32) - loss_target)
        return 0.5 * _jnp.sum(_jnp.mean(err, axis=-1)) if err.ndim else 0.5 * err


def _adamw(w, g, m, v):
    m = ADAM_B1 * m + (1.0 - ADAM_B1) * g
    v = ADAM_B2 * v + (1.0 - ADAM_B2) * _jnp.square(g)
    m_hat = m / (1.0 - ADAM_B1 ** ADAM_STEP)
    v_hat = v / (1.0 - ADAM_B2 ** ADAM_STEP)
    delta = -ADAM_LR * (m_hat / (_jnp.sqrt(v_hat) + ADAM_EPS) + ADAM_WD * w)
    return delta, m, v


def reference(x, mix_pre_norm, mix_post_norm, w_in, dn_conv, dn_A_log, dn_dt_bias, dn_out_norm, ret_out_norm, pool_w, pool_scale, w_branch_dn, w_branch_ret, w_branch_pool, w_out, ffn_pre_norm, ffn_post_norm, ffn_gate, ffn_up, ffn_down, loss_target, m_mix_pre_norm, m_mix_post_norm, m_w_in, m_dn_conv, m_dn_A_log, m_dn_dt_bias, m_dn_out_norm, m_ret_out_norm, m_pool_w, m_pool_scale, m_w_branch_dn, m_w_branch_ret, m_w_branch_pool, m_w_out, m_ffn_pre_norm, m_ffn_post_norm, m_ffn_gate, m_ffn_up, m_ffn_down, v_mix_pre_norm, v_mix_post_norm, v_w_in, v_dn_conv, v_dn_A_log, v_dn_dt_bias, v_dn_out_norm, v_ret_out_norm, v_pool_w, v_pool_scale, v_w_branch_dn, v_w_branch_ret, v_w_branch_pool, v_w_out, v_ffn_pre_norm, v_ffn_post_norm, v_ffn_gate, v_ffn_up, v_ffn_down):
    given = dict(x=x, mix_pre_norm=mix_pre_norm, mix_post_norm=mix_post_norm, w_in=w_in, dn_conv=dn_conv, dn_A_log=dn_A_log, dn_dt_bias=dn_dt_bias, dn_out_norm=dn_out_norm, ret_out_norm=ret_out_norm, pool_w=pool_w, pool_scale=pool_scale, w_branch_dn=w_branch_dn, w_branch_ret=w_branch_ret, w_branch_pool=w_branch_pool, w_out=w_out, ffn_pre_norm=ffn_pre_norm, ffn_post_norm=ffn_post_norm, ffn_gate=ffn_gate, ffn_up=ffn_up, ffn_down=ffn_down, loss_target=loss_target, m_mix_pre_norm=m_mix_pre_norm, m_mix_post_norm=m_mix_post_norm, m_w_in=m_w_in, m_dn_conv=m_dn_conv, m_dn_A_log=m_dn_A_log, m_dn_dt_bias=m_dn_dt_bias, m_dn_out_norm=m_dn_out_norm, m_ret_out_norm=m_ret_out_norm, m_pool_w=m_pool_w, m_pool_scale=m_pool_scale, m_w_branch_dn=m_w_branch_dn, m_w_branch_ret=m_w_branch_ret, m_w_branch_pool=m_w_branch_pool, m_w_out=m_w_out, m_ffn_pre_norm=m_ffn_pre_norm, m_ffn_post_norm=m_ffn_post_norm, m_ffn_gate=m_ffn_gate, m_ffn_up=m_ffn_up, m_ffn_down=m_ffn_down, v_mix_pre_norm=v_mix_pre_norm, v_mix_post_norm=v_mix_post_norm, v_w_in=v_w_in, v_dn_conv=v_dn_conv, v_dn_A_log=v_dn_A_log, v_dn_dt_bias=v_dn_dt_bias, v_dn_out_norm=v_dn_out_norm, v_ret_out_norm=v_ret_out_norm, v_pool_w=v_pool_w, v_pool_scale=v_pool_scale, v_w_branch_dn=v_w_branch_dn, v_w_branch_ret=v_w_branch_ret, v_w_branch_pool=v_w_branch_pool, v_w_out=v_w_out, v_ffn_pre_norm=v_ffn_pre_norm, v_ffn_post_norm=v_ffn_post_norm, v_ffn_gate=v_ffn_gate, v_ffn_up=v_ffn_up, v_ffn_down=v_ffn_down)
    weights = {n: given[n] for n in TWIN_WEIGHTS}
    shared = {n: given[n] for n in SHARED_INPUTS}
    per_example = {n: given[n] for n in ['x']}
    grad_fn = _jax.value_and_grad(_loss, argnums=(0, 1))

    def one_microbatch(ex, loss_target):
        ex = dict(ex)
        diff = ex.pop(TWIN_DIFF_INPUT)
        return grad_fn(weights, diff, {**shared, **ex}, loss_target)

    if N_MICROBATCH == 1:
        loss, (grad_w, grad_x) = one_microbatch(per_example, given["loss_target"])
    else:
        def body(carry, xs):
            loss_sum, grad_sum = carry
            l_k, (gw_k, gx_k) = one_microbatch(xs[0], xs[1])
            with _jax.named_scope("update"):
                return (loss_sum + l_k, _jax.tree.map(_jnp.add, grad_sum, gw_k)), gx_k

        init = (_jnp.zeros((), _jnp.float32), _jax.tree.map(_jnp.zeros_like, weights))
        (loss, grad_w), grad_x = _jax.lax.scan(body, init, (per_example, given["loss_target"]))
    with _jax.named_scope("update"):
        delta_w, new_m, new_v = {}, {}, {}
        for n in TWIN_WEIGHTS:
            delta_w[n], new_m[n], new_v[n] = _adamw(weights[n], grad_w[n], given["m_" + n], given["v_" + n])
    return (loss, grad_x, *[grad_w[n] for n in TWIN_WEIGHTS], *[delta_w[n] for n in TWIN_WEIGHTS],
            *[new_m[n] for n in TWIN_WEIGHTS], *[new_v[n] for n in TWIN_WEIGHTS])
```

```python
import functools

import numpy as np
import jax
import jax.numpy as jnp
from jax import lax
from jax.experimental import pallas as pl
from jax.experimental.pallas import tpu as pltpu

F32 = jnp.float32
MXU_DTYPE = jnp.bfloat16
WIRE_DTYPE = jnp.bfloat16
HI = lax.Precision.HIGHEST
SDS = jax.ShapeDtypeStruct
MESH = pl.DeviceIdType.MESH

CHUNK = 64
HEADS = 4
HEAD_DIM = 128
BRANCH_W = HEADS * HEAD_DIM
TAIL_W = 128
GROUP_ROWS = 512
RMS_EPS = 1e-6
GN_EPS = 1e-5
ROPE_BASE = 10000.0
POOL_WINDOWS = (2, 4, 8, 16)
PACK_W = 1024
VMEM_LIMIT_BYTES = 56 * 1024 * 1024

ADAM_LR = 0.001
ADAM_B1 = 0.9
ADAM_B2 = 0.999
ADAM_EPS = 1e-08
ADAM_WD = 0.01
ADAM_STEP = 10


def _params(sem=None):
    return pltpu.CompilerParams(dimension_semantics=sem, vmem_limit_bytes=VMEM_LIMIT_BYTES)


def _mx(v):
    return v.astype(MXU_DTYPE)


def _mm(name, a, b, *, ta=False, tb=False, contract_batch=False, out_dtype=F32, tm=1024, tn=None, tk=None):
    a3 = a if a.ndim == 3 else a[None]
    b3 = b if b.ndim == 3 else b[None]
    ba, bb = a3.shape[0], b3.shape[0]
    nb = max(ba, bb)
    k_dim, m_dim = (a3.shape[1], a3.shape[2]) if ta else (a3.shape[2], a3.shape[1])
    n_dim = b3.shape[1] if tb else b3.shape[2]
    assert k_dim == (b3.shape[2] if tb else b3.shape[1])
    tm = min(tm, m_dim)
    tn = n_dim if tn is None else min(tn, n_dim)
    tk = k_dim if tk is None else min(tk, k_dim)
    assert m_dim % tm == 0 and n_dim % tn == 0 and k_dim % tk == 0, (name, m_dim, n_dim, k_dim, tm, tn, tk)
    nbo, nbk = (1, nb) if contract_batch else (nb, 1)
    nk = k_dim // tk
    grid = (nbo, m_dim // tm, n_dim // tn, nbk, nk)

    def bsel(has, bo, kb):
        return (kb if contract_batch else bo) if has > 1 else 0

    a_spec = pl.BlockSpec((1, tk, tm) if ta else (1, tm, tk),
                          lambda bo, i, j, kb, kk: (bsel(ba, bo, kb),) + ((kk, i) if ta else (i, kk)))
    b_spec = pl.BlockSpec((1, tn, tk) if tb else (1, tk, tn),
                          lambda bo, i, j, kb, kk: (bsel(bb, bo, kb),) + ((j, kk) if tb else (kk, j)))
    o_spec = pl.BlockSpec((1, tm, tn), lambda bo, i, j, kb, kk: (bo, i, j))
    dims = (((0 if ta else 1,), (1 if tb else 0,)), ((), ()))
    steps = nbk * nk

    def body(a_ref, b_ref, o_ref, acc_ref):
        part = lax.dot_general(_mx(a_ref[0]), _mx(b_ref[0]), dims, preferred_element_type=F32)
        if steps == 1:
            o_ref[0] = part.astype(o_ref.dtype)
        else:
            step = pl.program_id(3) * nk + pl.program_id(4)

            @pl.when(step == 0)
            def _():
                acc_ref[...] = part

            @pl.when(step > 0)
            def _():
                acc_ref[...] += part

            @pl.when(step == steps - 1)
            def _():
                o_ref[0] = acc_ref[...].astype(o_ref.dtype)

    out = pl.pallas_call(
        body, name=name, grid=grid, in_specs=[a_spec, b_spec], out_specs=o_spec,
        out_shape=SDS((nbo, m_dim, n_dim), out_dtype),
        scratch_shapes=[pltpu.VMEM((tm, tn) if steps > 1 else (8, 128), F32)],
        compiler_params=_params(("parallel", "parallel", "parallel", "arbitrary", "arbitrary")),
    )(a3, b3)
    return out if nbo > 1 else out[0]


def _rowwise(name, fn, rows, fulls, out_rows, out_accs=(), *, tile):
    s = rows[0][0].shape[-2]
    tile = min(tile, s)
    assert s % tile == 0
    n_rows, n_full, n_or = len(rows), len(fulls), len(out_rows)

    def rspec(lead, width, cb):
        nl = len(lead)
        return pl.BlockSpec(tuple(lead) + (tile, width), lambda i, nl=nl, cb=cb: (0,) * nl + (i, cb))

    def fspec(shape):
        nd = len(shape)
        return pl.BlockSpec(tuple(shape), lambda i, nd=nd: (0,) * nd)

    in_specs = [rspec(arr.shape[:-2], w, cb) for arr, w, cb in rows] + [fspec(f.shape) for f in fulls]
    out_specs = [rspec(lead, w, 0) for lead, w, _ in out_rows] + [fspec(shape) for shape, _ in out_accs]
    out_shape = [SDS(tuple(lead) + (s, w), dt) for lead, w, dt in out_rows] + [SDS(tuple(shape), dt) for shape, dt in out_accs]

    def body(*refs):
        ins = [r[...] for r in refs[:n_rows + n_full]]
        outs = fn(*ins)
        o_refs = refs[n_rows + n_full:]
        for o_ref, val in zip(o_refs[:n_or], outs[:n_or]):
            o_ref[...] = val.astype(o_ref.dtype)
        first = pl.program_id(0) == 0
        for a_ref, val in zip(o_refs[n_or:], outs[n_or:]):
            @pl.when(first)
            def _(a_ref=a_ref, val=val):
                a_ref[...] = val.astype(a_ref.dtype)

            @pl.when(jnp.logical_not(first))
            def _(a_ref=a_ref, val=val):
                a_ref[...] += val.astype(a_ref.dtype)

    return pl.pallas_call(
        body, name=name, grid=(s // tile,), in_specs=in_specs, out_specs=out_specs, out_shape=out_shape,
        compiler_params=_params(("arbitrary",)),
    )(*[r[0] for r in rows], *fulls)


def _rms(x, w):
    return x * lax.rsqrt(jnp.mean(x * x, axis=-1, keepdims=True) + RMS_EPS) * w


def _silu(x):
    return x * jax.nn.sigmoid(x)


def _softplus(x):
    return jnp.maximum(x, 0.0) + jnp.log(1.0 + jnp.exp(-jnp.abs(x)))


def _pick_lane(row, idx):
    lane = lax.broadcasted_iota(jnp.int32, row.shape, row.ndim - 1)
    return jnp.sum(jnp.where(lane == idx, row, 0.0), axis=-1, keepdims=True)


def _pick_row(mat, idx):
    r = lax.broadcasted_iota(jnp.int32, mat.shape, 0)
    return jnp.sum(jnp.where(r == idx, mat, 0.0), axis=0, keepdims=True)


def _shift_rows(x, s):
    n = x.shape[0]
    rolled = pltpu.roll(x, s % n, 0)
    row = lax.broadcasted_iota(jnp.int32, x.shape, 0)
    keep = row >= s if s > 0 else row < n + s
    return jnp.where(keep, rolled, 0.0)


def _make_shift(s):
    @jax.custom_vjp
    def shift(x):
        return _shift_rows(x, s)

    shift.defvjp(lambda x: (_shift_rows(x, s), None), lambda _, g: (_shift_rows(g, -s),))
    return shift


_SHIFT = {s: _make_shift(s) for s in (1, 2, 3, 4, 8)}


@jax.custom_vjp
def _swap_halves(x):
    return pltpu.roll(x, HEAD_DIM // 2, 1)


_swap_halves.defvjp(lambda x: (pltpu.roll(x, HEAD_DIM // 2, 1), None), lambda _, g: (pltpu.roll(g, HEAD_DIM // 2, 1),))


def _bdot(a, b, ca, cb, precision=None):
    return lax.dot_general(a, b, (((ca,), (cb,)), ((0,), (0,))), precision=precision, preferred_element_type=F32)


def _dot(a, b, ca=1, cb=0):
    return lax.dot_general(a, b, (((ca,), (cb,)), ((), ())), preferred_element_type=F32)


def _conv_math(x, w0, w1, w2, w3):
    y = x * w3 + _SHIFT[1](x) * w2 + _SHIFT[2](x) * w1 + _SHIFT[3](x) * w0
    return _silu(y)


def _conv_fwd(projm, convw, base_block):
    s = projm.shape[0]
    nblk = convw.shape[1] // HEAD_DIM

    def body(x_ref, w_ref, o_ref):
        o_ref[...] = _conv_math(x_ref[...], *[w_ref[k:k + 1, :] for k in range(4)])

    return pl.pallas_call(
        body, name="dn_conv_fwd", grid=(nblk,),
        in_specs=[pl.BlockSpec((s, HEAD_DIM), lambda j: (0, base_block + j)), pl.BlockSpec((4, HEAD_DIM), lambda j: (0, j))],
        out_specs=pl.BlockSpec((s, HEAD_DIM), lambda j: (0, j)),
        out_shape=SDS((s, convw.shape[1]), F32), compiler_params=_params(("parallel",)),
    )(projm, convw)


def _conv_bwd(projm, convw, dy, base_block):
    s = projm.shape[0]
    nblk = convw.shape[1] // HEAD_DIM

    def body(x_ref, w_ref, dy_ref, dx_ref, dw_ref):
        _, vjp = jax.vjp(_conv_math, x_ref[...], *[w_ref[k:k + 1, :] for k in range(4)])
        grads = vjp(dy_ref[...])
        dx_ref[...] = grads[0]
        for k in range(4):
            dw_ref[k:k + 1, :] = jnp.sum(grads[1 + k], axis=0, keepdims=True)

    return pl.pallas_call(
        body, name="dn_conv_bwd", grid=(nblk,),
        in_specs=[pl.BlockSpec((s, HEAD_DIM), lambda j: (0, base_block + j)), pl.BlockSpec((4, HEAD_DIM), lambda j: (0, j)),
                  pl.BlockSpec((s, HEAD_DIM), lambda j: (0, j))],
        out_specs=[pl.BlockSpec((s, HEAD_DIM), lambda j: (0, j)), pl.BlockSpec((4, HEAD_DIM), lambda j: (0, j))],
        out_shape=[SDS((s, convw.shape[1]), F32), SDS(convw.shape, F32)], compiler_params=_params(("parallel",)),
    )(projm, convw, dy)


def _pool_math(grp, u, w, scale):
    w2 = u + _SHIFT[1](u)
    w4 = w2 + _SHIFT[2](w2)
    w8 = w4 + _SHIFT[4](w4)
    w16 = w8 + _SHIFT[8](w8)
    t1 = (lax.broadcasted_iota(jnp.int32, (u.shape[0], 1), 0) + 1).astype(F32)
    pooled = w16 / jnp.minimum(t1, float(POOL_WINDOWS[3]))
    for gi, acc in ((2, w8), (1, w4), (0, w2)):
        pooled = jnp.where(grp == gi, acc / jnp.minimum(t1, float(POOL_WINDOWS[gi])), pooled)
    mixed = pooled - u
    return _dot(_mx(mixed), _mx(w)) * scale


def _pool_fwd(projm, pool_w, pool_scale, base_block):
    s = projm.shape[0]

    def body(u_ref, w_ref, s_ref, o_ref):
        o_ref[...] = _pool_math(pl.program_id(0), u_ref[...], w_ref[0], s_ref[...]).astype(o_ref.dtype)

    return pl.pallas_call(
        body, name="pool_fwd", grid=(HEADS,),
        in_specs=[pl.BlockSpec((s, HEAD_DIM), lambda j: (0, base_block + j)), pl.BlockSpec((1, HEAD_DIM, HEAD_DIM), lambda j: (j, 0, 0)),
                  pl.BlockSpec((1, HEAD_DIM), lambda j: (0, j))],
        out_specs=pl.BlockSpec((s, HEAD_DIM), lambda j: (0, j)),
        out_shape=SDS((s, BRANCH_W), MXU_DTYPE), compiler_params=_params(("parallel",)),
    )(projm, pool_w, pool_scale)


def _pool_bwd(projm, pool_w, pool_scale, do, base_block):
    s = projm.shape[0]

    def body(u_ref, w_ref, s_ref, do_ref, du_ref, dw_ref, ds_ref):
        _, vjp = jax.vjp(functools.partial(_pool_math, pl.program_id(0)), u_ref[...], w_ref[0], s_ref[...])
        du, dw, ds = vjp(do_ref[...])
        du_ref[...] = du
        dw_ref[0] = dw
        ds_ref[...] = ds

    return pl.pallas_call(
        body, name="pool_bwd", grid=(HEADS,),
        in_specs=[pl.BlockSpec((s, HEAD_DIM), lambda j: (0, base_block + j)), pl.BlockSpec((1, HEAD_DIM, HEAD_DIM), lambda j: (j, 0, 0)),
                  pl.BlockSpec((1, HEAD_DIM), lambda j: (0, j)), pl.BlockSpec((s, HEAD_DIM), lambda j: (0, j))],
        out_specs=[pl.BlockSpec((s, HEAD_DIM), lambda j: (0, j)), pl.BlockSpec((1, HEAD_DIM, HEAD_DIM), lambda j: (j, 0, 0)),
                   pl.BlockSpec((1, HEAD_DIM), lambda j: (0, j))],
        out_shape=[SDS((s, BRANCH_W), F32), SDS(pool_w.shape, F32), SDS(pool_scale.shape, F32)],
        compiler_params=_params(("parallel",)),
    )(projm, pool_w, pool_scale, do)


def _scan_specs(s, rows, reverse):
    ng = s // GROUP_ROWS

    def gi(i):
        return ng - 1 - i if reverse else i

    specs = []
    for _, base, per_head in rows:
        specs.append(pl.BlockSpec((GROUP_ROWS, HEAD_DIM), lambda i, h, base=base, ph=per_head: (gi(i), base + (h if ph else 0))))
    return specs, gi, ng


def _scan_fwd(name, group_fn, rows, params, s):
    specs, gi, ng = _scan_specs(s, rows, False)
    n_in = len(rows) + len(params)
    pspecs = [pl.BlockSpec(p.shape, lambda i, h, nd=p.ndim: (0,) * nd) for p in params]

    def body(*refs):
        ins = [r[...] for r in refs[:n_in]]
        o_ref, saved_ref, st_ref = refs[n_in:]
        g, h = pl.program_id(0), pl.program_id(1)

        @pl.when(g == 0)
        def _():
            st_ref[h] = jnp.zeros((HEAD_DIM, HEAD_DIM), F32)

        state = st_ref[h]
        saved_ref[0, 0] = state
        out, new_state = group_fn(h, *ins, state)
        o_ref[...] = out.astype(o_ref.dtype)
        st_ref[h] = new_state

    return pl.pallas_call(
        body, name=name, grid=(ng, HEADS), in_specs=specs + pspecs,
        out_specs=[pl.BlockSpec((GROUP_ROWS, HEAD_DIM), lambda i, h: (i, h)),
                   pl.BlockSpec((1, 1, HEAD_DIM, HEAD_DIM), lambda i, h: (i, h, 0, 0))],
        out_shape=[SDS((s, BRANCH_W), MXU_DTYPE), SDS((ng, HEADS, HEAD_DIM, HEAD_DIM), F32)],
        scratch_shapes=[pltpu.VMEM((HEADS, HEAD_DIM, HEAD_DIM), F32)],
        compiler_params=_params(("arbitrary", "arbitrary")),
    )(*[r[0] for r in rows], *params)


def _scan_bwd(name, group_fn, rows, n_diff, params, saved, do, s):
    specs, gi, ng = _scan_specs(s, rows, True)
    n_rows, n_par = len(rows), len(params)
    pspecs = [pl.BlockSpec(p.shape, lambda i, h, nd=p.ndim: (0,) * nd) for p in params]
    in_specs = specs + pspecs + [pl.BlockSpec((1, 1, HEAD_DIM, HEAD_DIM), lambda i, h: (gi(i), h, 0, 0)),
                                 pl.BlockSpec((GROUP_ROWS, HEAD_DIM), lambda i, h: (gi(i), h))]
    out_specs, out_shape = [], []
    for _, _, per_head in rows[:n_diff]:
        out_specs.append(pl.BlockSpec((GROUP_ROWS, HEAD_DIM), lambda i, h, ph=per_head: (gi(i), h if ph else 0)))
        out_shape.append(SDS((s, BRANCH_W if per_head else HEAD_DIM), F32))
    for p in params:
        out_specs.append(pl.BlockSpec(p.shape, lambda i, h, nd=p.ndim: (0,) * nd))
        out_shape.append(SDS(p.shape, F32))

    def body(*refs):
        row_vals = [r[...] for r in refs[:n_rows]]
        par_vals = [r[...] for r in refs[n_rows:n_rows + n_par]]
        saved_ref, do_ref = refs[n_rows + n_par:n_rows + n_par + 2]
        outs = refs[n_rows + n_par + 2:-1]
        dst_ref = refs[-1]
        i, h = pl.program_id(0), pl.program_id(1)

        @pl.when(i == 0)
        def _():
            dst_ref[h] = jnp.zeros((HEAD_DIM, HEAD_DIM), F32)

        consts = row_vals[n_diff:]

        def f(*args):
            return group_fn(h, *args[:n_diff], *consts, *args[n_diff:])

        _, vjp = jax.vjp(f, *row_vals[:n_diff], *par_vals, saved_ref[0, 0])
        grads = vjp((do_ref[...], dst_ref[h]))
        dst_ref[h] = grads[-1]
        for k, (_, _, per_head) in enumerate(rows[:n_diff]):
            if per_head:
                outs[k][...] = grads[k]
            else:
                @pl.when(h == 0)
                def _(k=k):
                    outs[k][...] = grads[k]

                @pl.when(h > 0)
                def _(k=k):
                    outs[k][...] += grads[k]
        first = jnp.logical_and(i == 0, h == 0)
        for k in range(n_par):
            @pl.when(first)
            def _(k=k):
                outs[n_diff + k][...] = grads[n_diff + k]

            @pl.when(jnp.logical_not(first))
            def _(k=k):
                outs[n_diff + k][...] += grads[n_diff + k]

    return pl.pallas_call(
        body, name=name, grid=(ng, HEADS), in_specs=in_specs, out_specs=out_specs, out_shape=out_shape,
        scratch_shapes=[pltpu.VMEM((HEADS, HEAD_DIM, HEAD_DIM), F32)],
        compiler_params=_params(("arbitrary", "arbitrary")),
    )(*[r[0] for r in rows], *params, saved, do)


def _dn_group(h, qc, kc, vc, z, tail, alog, dtb, dnw, state):
    c = CHUNK
    nb = qc.shape[0] // c
    beta = jax.nn.sigmoid(_pick_lane(tail, h))
    g = -jnp.exp(_pick_lane(alog, h)) * _softplus(_pick_lane(tail, HEADS + h) + _pick_lane(dtb, h))
    q = qc * lax.rsqrt(jnp.sum(qc * qc, axis=-1, keepdims=True) + 1e-6) * (HEAD_DIM ** -0.5)
    k = kc * lax.rsqrt(jnp.sum(kc * kc, axis=-1, keepdims=True) + 1e-6)
    q, k, v = (t.reshape(nb, c, HEAD_DIM) for t in (q, k, vc))
    beta = beta.reshape(nb, c, 1)
    g = g.reshape(nb, c, 1)
    g_wide = jnp.broadcast_to(g, (nb, c, HEAD_DIM))
    g_sq = jnp.broadcast_to(g, (nb, c, c))
    ri = lax.broadcasted_iota(jnp.int32, (nb, c, c), 1)
    ci = lax.broadcasted_iota(jnp.int32, (nb, c, c), 2)
    causal = ri >= ci
    lower = causal.astype(F32)
    upper = (ri <= ci).astype(F32)
    cum_wide = _bdot(lower, g_wide, 2, 1, HI)
    cum_i = _bdot(lower, g_sq, 2, 1, HI)
    cum_j = _bdot(g_sq, upper, 1, 1, HI)
    decay = jnp.where(causal, jnp.exp(jnp.where(causal, cum_i - cum_j, 0.0)), 0.0)
    k_beta = k * beta
    m = jnp.where(ri > ci, _bdot(k_beta, k, 2, 2, HI) * decay, 0.0)
    p = -m
    inv = jnp.where(ri == ci, 1.0, 0.0) + p
    for _ in range(5):
        p = _bdot(p, p, 2, 1, HI)
        inv = inv + _bdot(inv, p, 2, 1, HI)
    e_cum = jnp.exp(cum_wide)
    u = _bdot(inv, v * beta, 2, 1, HI)
    w = _bdot(inv, k_beta * e_cum, 2, 1, HI)
    attn = jnp.where(causal, _bdot(q, k, 2, 2, HI) * decay, 0.0)
    q_dec = q * e_cum
    g_last = jnp.sum(g_wide, axis=1, keepdims=True)
    k_tail = k * jnp.exp(g_last - cum_wide)
    d_last = jnp.exp(g_last)
    outs = []
    for n in range(nb):
        st = _mx(state)
        v_new = u[n] - _dot(_mx(w[n]), st)
        outs.append(_dot(_mx(q_dec[n]), st) + _dot(_mx(attn[n]), _mx(v_new)))
        state = state * d_last[n] + _dot(_mx(k_tail[n]), _mx(v_new), 0, 0)
    o = jnp.concatenate(outs, axis=0)
    o = o * lax.rsqrt(jnp.mean(o * o, axis=-1, keepdims=True) + RMS_EPS) * dnw * _silu(z)
    return o, state


def _ret_group(h, q, k, v, gate, cosf, sinf, retw, lgam, state):
    c = CHUNK
    nb = q.shape[0] // c
    lg = _pick_lane(lgam, h)
    rq = q * cosf + _swap_halves(q) * sinf
    rk = (k * cosf + _swap_halves(k) * sinf) * (HEAD_DIM ** -0.5)
    rq, rk, v3 = (t.reshape(nb, c, HEAD_DIM) for t in (rq, rk, v))
    ri = lax.broadcasted_iota(jnp.int32, (nb, c, c), 1)
    ci = lax.broadcasted_iota(jnp.int32, (nb, c, c), 2)
    dmask = jnp.exp(jnp.abs(ri - ci).astype(F32) * lg)
    scores = _bdot(_mx(rq), _mx(rk), 2, 2) * dmask
    o_inner = _bdot(_mx(scores), _mx(v3), 2, 1)
    pos = lax.broadcasted_iota(jnp.int32, (nb, c, 1), 1).astype(F32)
    xi = jnp.exp((pos + 1.0) * lg)
    zeta = jnp.exp((c - 1.0 - pos) * lg)
    kv = _bdot(_mx(rk * zeta), _mx(v3), 1, 1)
    chunk_decay = jnp.exp(float(c) * lg)
    entering = []
    for n in range(nb):
        entering.append(state)
        state = state * chunk_decay + kv[n]
    r_prev = jnp.stack(entering, axis=0)
    o = (o_inner + _bdot(_mx(rq * xi), _mx(r_prev), 2, 1)).reshape(nb * c, HEAD_DIM)
    mu = jnp.mean(o, axis=-1, keepdims=True)
    var = jnp.mean(jnp.square(o - mu), axis=-1, keepdims=True)
    o = (o - mu) * lax.rsqrt(var + GN_EPS) * _pick_row(retw, h) * _silu(gate)
    return o, state


def _place():
    x, y, c = lax.axis_index("x"), lax.axis_index("y"), lax.axis_index("c")
    chips = [(x, 1 - y), (1 - x, y), (1 - x, 1 - y)]
    return x, y, c, chips


_HBM = pl.BlockSpec(memory_space=pltpu.HBM)


def _gather_weights(shard):
    r, wd = shard.shape
    hr = r // 2

    def body(x_ref, o_ref, send_sems, recv_sems, local_sem):
        x, y, c, chips = _place()
        sib = (x, y, 1 - c)
        mine_rows = pl.ds(pl.multiple_of(c * hr, 16), hr)
        sib_rows = pl.ds(pl.multiple_of((1 - c) * hr, 16), hr)

        def copy(sem, src, slot, rows, to):
            return pltpu.make_async_remote_copy(src_ref=src, dst_ref=o_ref.at[slot, rows], send_sem=send_sems.at[sem],
                                                recv_sem=recv_sems.at[sem], device_id=to, device_id_type=MESH)

        own = pltpu.make_async_copy(x_ref, o_ref.at[2 * x + y], local_sem)
        own.start()
        first = [copy(j, x_ref.at[mine_rows], 2 * x + y, mine_rows, (*chip, c)) for j, chip in enumerate(chips)]
        for cp in first:
            cp.start()
        passed = [copy(3 + j, o_ref.at[2 * chip[0] + chip[1], mine_rows], 2 * chip[0] + chip[1], mine_rows, sib)
                  for j, chip in enumerate(chips)]
        for j, chip in enumerate(chips):
            copy(j, x_ref.at[mine_rows], 2 * chip[0] + chip[1], mine_rows, (*chip, c)).wait_recv()
            passed[j].start()
        for j, chip in enumerate(chips):
            copy(3 + j, x_ref.at[mine_rows], 2 * chip[0] + chip[1], sib_rows, sib).wait_recv()
        for cp in first + passed:
            cp.wait_send()
        own.wait()

    return pl.pallas_call(
        body, name="gather_weights", out_shape=SDS((4, r, wd), shard.dtype), in_specs=[_HBM], out_specs=_HBM,
        scratch_shapes=[pltpu.SemaphoreType.DMA((6,)), pltpu.SemaphoreType.DMA((6,)), pltpu.SemaphoreType.DMA],
    )(shard)


def _swap_with_sibling(grads):
    _, r, wd = grads.shape
    hr = r // 2

    def body(g_ref, o_ref, send_sem, recv_sem):
        x, y, c, _ = _place()
        sib_rows = pl.ds(pl.multiple_of((1 - c) * hr, 16), hr)
        cp = pltpu.make_async_remote_copy(src_ref=g_ref.at[:, sib_rows], dst_ref=o_ref, send_sem=send_sem, recv_sem=recv_sem,
                                          device_id=(x, y, 1 - c), device_id_type=MESH)
        cp.start()
        cp.wait()

    return pl.pallas_call(
        body, name="grad_to_sibling", out_shape=SDS((4, hr, wd), grads.dtype), in_specs=[_HBM], out_specs=_HBM,
        scratch_shapes=[pltpu.SemaphoreType.DMA, pltpu.SemaphoreType.DMA],
    )(grads)


def _scatter_to_chips(part):
    _, hr, wd = part.shape

    def body(p_ref, o_ref, send_sems, recv_sems):
        x, y, c, chips = _place()
        copies = [pltpu.make_async_remote_copy(src_ref=p_ref.at[2 * chip[0] + chip[1]], dst_ref=o_ref.at[j], send_sem=send_sems.at[j],
                                               recv_sem=recv_sems.at[j], device_id=(*chip, c), device_id_type=MESH)
                  for j, chip in enumerate(chips)]
        for cp in copies:
            cp.start()
        for cp in copies:
            cp.wait()

    return pl.pallas_call(
        body, name="grad_to_chips", out_shape=SDS((3, hr, wd), part.dtype), in_specs=[_HBM], out_specs=_HBM,
        scratch_shapes=[pltpu.SemaphoreType.DMA((3,)), pltpu.SemaphoreType.DMA((3,))],
    )(part)


def _join_halves(half):
    hr, wd = half.shape

    def body(h_ref, o_ref, send_sem, recv_sem, local_sem):
        x, y, c, _ = _place()
        mine_rows = pl.ds(pl.multiple_of(c * hr, 8), hr)
        sib_rows = pl.ds(pl.multiple_of((1 - c) * hr, 8), hr)
        own = pltpu.make_async_copy(h_ref, o_ref.at[mine_rows], local_sem)
        own.start()
        cp = pltpu.make_async_remote_copy(src_ref=h_ref, dst_ref=o_ref.at[mine_rows], send_sem=send_sem, recv_sem=recv_sem,
                                          device_id=(x, y, 1 - c), device_id_type=MESH)
        cp.start()
        pltpu.make_async_remote_copy(src_ref=h_ref, dst_ref=o_ref.at[sib_rows], send_sem=send_sem, recv_sem=recv_sem,
                                     device_id=(x, y, 1 - c), device_id_type=MESH).wait_recv()
        cp.wait_send()
        own.wait()

    return pl.pallas_call(
        body, name="grad_join_halves", out_shape=SDS((2 * hr, wd), half.dtype), in_specs=[_HBM], out_specs=_HBM,
        scratch_shapes=[pltpu.SemaphoreType.DMA, pltpu.SemaphoreType.DMA, pltpu.SemaphoreType.DMA],
    )(half)


def _reduce_scatter(grads):
    _, r, wd = grads.shape
    hr = r // 2
    tile = 336 if hr % 336 == 0 else 16
    nt = hr // tile
    c = lax.axis_index("c").astype(jnp.int32).reshape(1)
    k = (2 * lax.axis_index("x") + lax.axis_index("y")).astype(jnp.int32).reshape(1)
    from_sib = _swap_with_sibling(grads)

    def add_sib(c_ref, g_ref, s_ref, o_ref):
        o_ref[...] = (g_ref[...].astype(F32) + s_ref[...].astype(F32)).astype(o_ref.dtype)

    part = pl.pallas_call(
        add_sib, name="grad_add_sibling",
        grid_spec=pltpu.PrefetchScalarGridSpec(
            num_scalar_prefetch=1, grid=(nt,),
            in_specs=[pl.BlockSpec((4, tile, wd), lambda i, c_ref: (0, c_ref[0] * nt + i, 0)),
                      pl.BlockSpec((4, tile, wd), lambda i, c_ref: (0, i, 0))],
            out_specs=pl.BlockSpec((4, tile, wd), lambda i, c_ref: (0, i, 0))),
        out_shape=SDS((4, hr, wd), grads.dtype), compiler_params=_params(("parallel",)),
    )(c, grads, from_sib)
    from_chips = _scatter_to_chips(part)

    def add_chips(k_ref, p_ref, r_ref, o_ref):
        o_ref[...] = ((p_ref[0].astype(F32) + r_ref[0].astype(F32)) + r_ref[1].astype(F32)) + r_ref[2].astype(F32)

    half = pl.pallas_call(
        add_chips, name="grad_add_chips",
        grid_spec=pltpu.PrefetchScalarGridSpec(
            num_scalar_prefetch=1, grid=(nt,),
            in_specs=[pl.BlockSpec((1, tile, wd), lambda i, k_ref: (k_ref[0], i, 0)),
                      pl.BlockSpec((3, tile, wd), lambda i, k_ref: (0, i, 0))],
            out_specs=pl.BlockSpec((tile, wd), lambda i, k_ref: (i, 0))),
        out_shape=SDS((hr, wd), F32), compiler_params=_params(("parallel",)),
    )(k, part, from_chips)
    return _join_halves(half)


def _all_reduce_small(x):
    _, n, wd = x.shape

    def body(x_ref, o_ref, buf, send1, recv1, send2, recv2):
        mx, my, mc = lax.axis_index("x"), lax.axis_index("y"), lax.axis_index("c")
        me = 4 * mx + 2 * my + mc
        peers = []
        for rel in range(1, 8):
            px, py, pc = (1 - mx if rel & 4 else mx), (1 - my if rel & 2 else my), (1 - mc if rel & 1 else mc)
            peers.append(((px, py, pc), 4 * px + 2 * py + pc))
        out = [pltpu.make_async_remote_copy(src_ref=x_ref.at[pidx], dst_ref=buf.at[j], send_sem=send1.at[j], recv_sem=recv1.at[j],
                                            device_id=peer, device_id_type=MESH) for j, (peer, pidx) in enumerate(peers)]
        for cp in out:
            cp.start()
        for cp in out:
            cp.wait()
        acc = x_ref[me]
        for j in range(7):
            acc = acc + buf[j]
        o_ref[me] = acc
        back = [pltpu.make_async_remote_copy(src_ref=o_ref.at[me], dst_ref=o_ref.at[me], send_sem=send2.at[j], recv_sem=recv2.at[j],
                                             device_id=peer, device_id_type=MESH) for j, (peer, _) in enumerate(peers)]
        for cp in back:
            cp.start()
        for j, (peer, pidx) in enumerate(peers):
            pltpu.make_async_remote_copy(src_ref=o_ref.at[me], dst_ref=o_ref.at[pidx], send_sem=send2.at[j], recv_sem=recv2.at[j],
                                         device_id=peer, device_id_type=MESH).wait_recv()
        for cp in back:
            cp.wait_send()

    vm = pl.BlockSpec(memory_space=pltpu.VMEM)
    return pl.pallas_call(
        body, name="all_reduce_small", out_shape=SDS(x.shape, F32), in_specs=[vm], out_specs=vm,
        scratch_shapes=[pltpu.VMEM((7, n, wd), F32)] + [pltpu.SemaphoreType.DMA((7,))] * 4,
        compiler_params=pltpu.CompilerParams(vmem_limit_bytes=VMEM_LIMIT_BYTES),
    )(x)


def _to_exchange_buffer(flat):
    n = -(-flat.shape[0] // (8 * 8 * 128)) * 8
    return jnp.pad(flat, (0, 8 * n * 128 - flat.shape[0])).reshape(8, n, 128)


def _pack(arrays, dtype):
    flat = jnp.concatenate([a.astype(dtype).reshape(-1) for a in arrays])
    rows = -(-flat.shape[0] // PACK_W)
    rows = -(-rows // 32) * 32
    return jnp.pad(flat, (0, rows * PACK_W - flat.shape[0])).reshape(rows, PACK_W)


def _unpack(buf, shapes):
    lead = buf.shape[:-2]
    flat = buf.reshape(lead + (-1,))
    out, off = [], 0
    for shp in shapes:
        n = int(np.prod(shp))
        out.append(flat[..., off:off + n].reshape(lead + tuple(shp)))
        off += n
    return out


def _adamw_math(w, g, m, v):
    m = ADAM_B1 * m + (1.0 - ADAM_B1) * g
    v = ADAM_B2 * v + (1.0 - ADAM_B2) * jnp.square(g)
    m_hat = m / (1.0 - ADAM_B1 ** ADAM_STEP)
    v_hat = v / (1.0 - ADAM_B2 ** ADAM_STEP)
    delta = -ADAM_LR * (m_hat / (jnp.sqrt(v_hat) + ADAM_EPS) + ADAM_WD * w)
    return delta, m, v


def _adamw_large(name, w, g, m, v):
    shape = w.shape
    cols = shape[-1]
    rows = int(np.prod(shape[:-1]))
    tile = 256 if rows % 256 == 0 else rows
    args = [t.reshape(rows, cols) for t in (w, g, m, v)]
    outs = _rowwise(name, _adamw_math, [(a, cols, 0) for a in args], [], [((), cols, F32)] * 3, tile=tile)
    return [o.reshape(shape) for o in outs]


def _adamw_small(ws, gs, ms, vs):
    n = len(ws)

    def body(*refs):
        for i in range(n):
            d, m, v = _adamw_math(refs[i][...], refs[n + i][...], refs[2 * n + i][...], refs[3 * n + i][...])
            refs[4 * n + i][...] = d
            refs[5 * n + i][...] = m
            refs[6 * n + i][...] = v

    outs = pl.pallas_call(body, name="adamw_small", out_shape=[SDS(w.shape, F32) for w in ws] * 3,
                          compiler_params=pltpu.CompilerParams(vmem_limit_bytes=VMEM_LIMIT_BYTES))(*ws, *gs, *ms, *vs)
    return outs[:n], outs[n:2 * n], outs[2 * n:]


def _rotary_tables(s):
    half = HEAD_DIM // 2
    inv = ROPE_BASE ** (-jnp.arange(half, dtype=F32) / half)
    ang = jnp.arange(s, dtype=F32)[:, None] * inv[None, :]
    cos, sin = jnp.cos(ang), jnp.sin(ang)
    return jnp.concatenate([cos, cos], axis=-1), jnp.concatenate([-sin, sin], axis=-1)


def kernel(x, mix_pre_norm, mix_post_norm, w_in, dn_conv, dn_A_log, dn_dt_bias, dn_out_norm, ret_out_norm, pool_w, pool_scale, w_branch_dn, w_branch_ret, w_branch_pool, w_out, ffn_pre_norm, ffn_post_norm, ffn_gate, ffn_up, ffn_down, loss_target, m_mix_pre_norm, m_mix_post_norm, m_w_in, m_dn_conv, m_dn_A_log, m_dn_dt_bias, m_dn_out_norm, m_ret_out_norm, m_pool_w, m_pool_scale, m_w_branch_dn, m_w_branch_ret, m_w_branch_pool, m_w_out, m_ffn_pre_norm, m_ffn_post_norm, m_ffn_gate, m_ffn_up, m_ffn_down, v_mix_pre_norm, v_mix_post_norm, v_w_in, v_dn_conv, v_dn_A_log, v_dn_dt_bias, v_dn_out_norm, v_ret_out_norm, v_pool_w, v_pool_scale, v_w_branch_dn, v_w_branch_ret, v_w_branch_pool, v_w_out, v_ffn_pre_norm, v_ffn_post_norm, v_ffn_gate, v_ffn_up, v_ffn_down):
    depth = w_in.shape[0]
    s, d = x.shape[1], x.shape[2]
    d_in4 = w_in.shape[2]
    d_in = 4 * d_in4
    f4 = ffn_gate.shape[2]
    qkv_w = 3 * BRANCH_W
    g0 = 3 * d
    b_qkv, b_z = g0 // HEAD_DIM, (g0 + qkv_w) // HEAD_DIM
    b_rq, b_rk, b_rv, b_rg, b_pu = ((g0 + qkv_w + BRANCH_W * i) // HEAD_DIM for i in range(1, 6))
    main_w = g0 + qkv_w + 6 * BRANCH_W
    src_tail = qkv_w + BRANCH_W
    assert d_in == main_w + 2 * HEADS
    col_tile = 1536 if main_w % 1536 == 0 else 128
    tile = 256
    xk = lax.axis_index("x") * 2 + lax.axis_index("y")

    big_shapes = [(d, d_in4), (BRANCH_W, d // 4), (BRANCH_W, d // 4), (BRANCH_W, d // 4), (d // 4, d), (d, f4), (d, f4), (f4, d)]
    cosf, sinf = _rotary_tables(s)
    lgam = jnp.zeros((1, HEAD_DIM), F32).at[0, :HEADS].set(jnp.log1p(-jnp.exp2(-5.0 - jnp.arange(HEADS, dtype=F32))))

    def pad_lanes(vec):
        return jnp.zeros((1, HEAD_DIM), F32).at[0, :vec.shape[0]].set(vec)

    gathered = []
    for l in range(depth):
        shard = _pack([w_in[l], w_branch_dn[l], w_branch_ret[l], w_branch_pool[l], w_out[l], ffn_gate[l], ffn_up[l], ffn_down[l]], WIRE_DTYPE)
        gathered.append(_gather_weights(shard))
    conv_place = lax.dynamic_update_slice(jnp.zeros((4,) + dn_conv.shape, F32), dn_conv[None], (xk, 0, 0, 0))
    conv_flat = conv_place.reshape(-1) * (1 - lax.axis_index("c")).astype(F32)
    conv_sum = _all_reduce_small(_to_exchange_buffer(conv_flat)).reshape(-1)[:conv_flat.shape[0]].reshape(conv_place.shape)
    conv_all = jnp.transpose(conv_sum, (1, 2, 0, 3)).reshape(depth, 4, qkv_w)

    def layer_weights(l):
        wi, wbd, wbr, wbp, wo, wg, wu, wdn = _unpack(gathered[l], big_shapes)
        full = jnp.transpose(wi, (1, 0, 2)).reshape(d, d_in)
        rest = src_tail + 2 * HEADS
        w_main = jnp.concatenate([full[:, rest + 5 * BRANCH_W:], full[:, :src_tail], full[:, rest:rest + 5 * BRANCH_W]], axis=1)
        w_tail = jnp.pad(full[:, src_tail:rest], ((0, 0), (0, TAIL_W - 2 * HEADS)))
        wb = [jnp.transpose(t, (1, 0, 2)).reshape(BRANCH_W, d) for t in (wbd, wbp, wbr)]
        return dict(main=w_main, tail=w_tail, wb=wb, out=wo.reshape(d, d), gate=wg, up=wu, down=wdn)

    def norm_fwd(name, xin, w):
        return _rowwise(name, lambda xv, wv: (_rms(xv, wv),), [(xin, d, 0)], [w], [((), d, MXU_DTYPE)], tile=tile)[0]

    def resnorm_fwd(name, xin, t, w):
        return _rowwise(name, lambda xv, tv, wv: (xv + _rms(tv, wv),), [(xin, d, 0), (t, d, 0)], [w], [((), d, F32)], tile=tile)[0]

    def merge_math(gates, p_dn, p_pool, p_ret):
        gt = jax.nn.sigmoid(gates)
        return gt[:, :d] * p_dn + gt[:, d:2 * d] * p_pool + gt[:, 2 * d:] * p_ret

    def swiglu_math(u, v):
        return _silu(u) * v

    xs, saved = [x[0]], []
    for l in range(depth):
        wts = layer_weights(l)
        xin = xs[-1]
        h = norm_fwd("mix_pre_norm_fwd", xin, mix_pre_norm[l][None])
        projm = _mm("proj_main", h, wts["main"], tn=col_tile)
        projt = _mm("proj_tail", h, wts["tail"])
        convw = conv_all[l]
        qkv_c = _conv_fwd(projm, convw, b_qkv)
        dn_par = [pad_lanes(dn_A_log[l]), pad_lanes(dn_dt_bias[l]), dn_out_norm[l][None]]
        dn_rows = [(qkv_c, 0, True), (qkv_c, HEADS, True), (qkv_c, 2 * HEADS, True), (projm, b_z, True), (projt, 0, False)]
        o_dn, dn_states = _scan_fwd("dn_fwd", _dn_group, dn_rows, dn_par, s)
        ret_par = [ret_out_norm[l].reshape(HEADS, HEAD_DIM), lgam]
        ret_rows = [(projm, b_rq, True), (projm, b_rk, True), (projm, b_rv, True), (projm, b_rg, True), (cosf, 0, False), (sinf, 0, False)]
        o_ret, ret_states = _scan_fwd("ret_fwd", _ret_group, ret_rows, ret_par, s)
        o_pool = _pool_fwd(projm, pool_w[l], pool_scale[l][None], b_pu)
        p_dn = _mm("branch_dn", o_dn, wts["wb"][0])
        p_pool = _mm("branch_pool", o_pool, wts["wb"][1])
        p_ret = _mm("branch_ret", o_ret, wts["wb"][2])
        ymix = _rowwise("merge_fwd", lambda *a: (merge_math(*a),), [(projm, 3 * d, 0), (p_dn, d, 0), (p_pool, d, 0), (p_ret, d, 0)], [],
                        [((), d, MXU_DTYPE)], tile=tile)[0]
        mo = _mm("mix_out", ymix, wts["out"])
        xmid = resnorm_fwd("mix_post_norm_fwd", xin, mo, mix_post_norm[l][None])
        h2 = norm_fwd("ffn_pre_norm_fwd", xmid, ffn_pre_norm[l][None])
        fu = _mm("ffn_gate", h2, wts["gate"])
        fv = _mm("ffn_up", h2, wts["up"])
        act = _rowwise("swiglu_fwd", lambda u, v: (swiglu_math(u, v),), [(fu, f4, 0), (fv, f4, 0)], [], [((4,), f4, MXU_DTYPE)], tile=tile)[0]
        fo = _mm("ffn_down", act, wts["down"], contract_batch=True)
        xout = resnorm_fwd("ffn_post_norm_fwd", xmid, fo, ffn_post_norm[l][None])
        xs.append(xout)
        saved.append(dict(wts=wts, h=h, projm=projm, convw=convw, dn_rows=dn_rows, dn_par=dn_par, dn_states=dn_states,
                          ret_rows=ret_rows, ret_par=ret_par, ret_states=ret_states, o_dn=o_dn, o_pool=o_pool, o_ret=o_ret,
                          p=(p_dn, p_pool, p_ret), ymix=ymix, mo=mo, xmid=xmid, h2=h2, fu=fu, fv=fv, act=act, fo=fo, xin=xin))

    def loss_math(y, t):
        diff = y - t
        part = 0.5 * jnp.sum(jnp.sum(diff * diff, axis=-1, keepdims=True) / d, axis=0, keepdims=True)
        return diff / d, jnp.broadcast_to(part, (8, 128))

    gx, loss_part = _rowwise("loss_head", loss_math, [(xs[-1], d, 0), (loss_target[0], d, 0)], [], [((), d, F32)], [((8, 128), F32)], tile=tile)
    loss = lax.psum(loss_part[0, 0], ("x", "y", "c"))

    def resnorm_bwd(name, t, gout, w):
        def fn(tv, gv, wv):
            _, vjp = jax.vjp(_rms, tv, wv)
            return vjp(gv)
        return _rowwise(name, fn, [(t, d, 0), (gout, d, 0)], [w], [((), d, F32)], [((1, d), F32)], tile=tile)

    def norm_bwd(name, xin, dh_a, dh_b, gout, w):
        def fn(xv, da, db, gv, wv):
            _, vjp = jax.vjp(_rms, xv, wv)
            dx_, dw_ = vjp(da + db)
            return gv + dx_, dw_
        return _rowwise(name, fn, [(xin, d, 0), (dh_a, d, 0), (dh_b, d, 0), (gout, d, 0)], [w], [((), d, F32)], [((1, d), F32)], tile=tile)

    def swiglu_bwd(u, v, da):
        _, vjp = jax.vjp(swiglu_math, u, v)
        return vjp(da)

    def merge_bwd(gates, p_dn, p_pool, p_ret, dyv):
        _, vjp = jax.vjp(merge_math, gates, p_dn, p_pool, p_ret)
        return vjp(dyv)

    big_grads, small_grads = [None] * depth, [None] * depth
    for l in reversed(range(depth)):
        sv = saved[l]
        wts = sv["wts"]
        dfo, d_fpost = resnorm_bwd("ffn_post_norm_bwd", sv["fo"], gx, ffn_post_norm[l][None])
        dact = _mm("ffn_down_dx", dfo, wts["down"], tb=True)
        g_down = _mm("ffn_down_dw", sv["act"], dfo, ta=True, out_dtype=WIRE_DTYPE, tm=f4)
        du, dv = _rowwise("swiglu_bwd", swiglu_bwd, [(sv["fu"], f4, 0), (sv["fv"], f4, 0), (dact, f4, 0)], [],
                          [((4,), f4, MXU_DTYPE), ((4,), f4, MXU_DTYPE)], tile=tile)
        g_gate = _mm("ffn_gate_dw", sv["h2"], du, ta=True, out_dtype=WIRE_DTYPE)
        g_up = _mm("ffn_up_dw", sv["h2"], dv, ta=True, out_dtype=WIRE_DTYPE)
        dh2_a = _mm("ffn_gate_dx", du, wts["gate"], tb=True, contract_batch=True)
        dh2_b = _mm("ffn_up_dx", dv, wts["up"], tb=True, contract_batch=True)
        gmid, d_fpre = norm_bwd("ffn_pre_norm_bwd", sv["xmid"], dh2_a, dh2_b, gx, ffn_pre_norm[l][None])
        dmo, d_post = resnorm_bwd("mix_post_norm_bwd", sv["mo"], gmid, mix_post_norm[l][None])
        dy = _mm("mix_out_dx", dmo, wts["out"], tb=True)
        g_out = _mm("mix_out_dw", sv["ymix"], dmo, ta=True, out_dtype=WIRE_DTYPE)
        dgates, dp_dn, dp_pool, dp_ret = _rowwise(
            "merge_bwd", merge_bwd, [(sv["projm"], 3 * d, 0), (sv["p"][0], d, 0), (sv["p"][1], d, 0), (sv["p"][2], d, 0), (dy, d, 0)], [],
            [((), 3 * d, MXU_DTYPE), ((), d, MXU_DTYPE), ((), d, MXU_DTYPE), ((), d, MXU_DTYPE)], tile=tile)
        do_dn = _mm("branch_dn_dx", dp_dn, wts["wb"][0], tb=True)
        do_pool = _mm("branch_pool_dx", dp_pool, wts["wb"][1], tb=True)
        do_ret = _mm("branch_ret_dx", dp_ret, wts["wb"][2], tb=True)
        g_wb_dn = _mm("branch_dn_dw", sv["o_dn"], dp_dn, ta=True, out_dtype=WIRE_DTYPE, tm=BRANCH_W)
        g_wb_pool = _mm("branch_pool_dw", sv["o_pool"], dp_pool, ta=True, out_dtype=WIRE_DTYPE, tm=BRANCH_W)
        g_wb_ret = _mm("branch_ret_dw", sv["o_ret"], dp_ret, ta=True, out_dtype=WIRE_DTYPE, tm=BRANCH_W)
        d_rq, d_rk, d_rv, d_rg, d_retw, _ = _scan_bwd("ret_bwd", _ret_group, sv["ret_rows"], 4, sv["ret_par"], sv["ret_states"], do_ret, s)
        d_pu, d_poolw, d_pools = _pool_bwd(sv["projm"], pool_w[l], pool_scale[l][None], do_pool, b_pu)
        d_qc, d_kc, d_vc, d_z, d_tail, d_alog, d_dtb, d_dnw = _scan_bwd("dn_bwd", _dn_group, sv["dn_rows"], 5, sv["dn_par"], sv["dn_states"], do_dn, s)
        d_qkv, d_convw = _conv_bwd(sv["projm"], sv["convw"], jnp.concatenate([d_qc, d_kc, d_vc], axis=1), b_qkv)
        dprojm = jnp.concatenate([dgates] + [t.astype(MXU_DTYPE) for t in (d_qkv, d_z, d_rq, d_rk, d_rv, d_rg, d_pu)], axis=1)
        dh_a = _mm("proj_main_dx", dprojm, wts["main"], tb=True, tk=col_tile)
        dh_b = _mm("proj_tail_dx", d_tail, wts["tail"], tb=True)
        g_main = _mm("proj_main_dw", sv["h"], dprojm, ta=True, out_dtype=WIRE_DTYPE, tn=col_tile)
        g_tail = _mm("proj_tail_dw", sv["h"], d_tail, ta=True, out_dtype=WIRE_DTYPE)
        gx, d_pre = norm_bwd("mix_pre_norm_bwd", sv["xin"], dh_a, dh_b, gmid, mix_pre_norm[l][None])
        g_full = jnp.concatenate([g_main[:, g0:g0 + src_tail], g_tail[:, :2 * HEADS], g_main[:, g0 + src_tail:], g_main[:, :g0]], axis=1)
        g_in = jnp.transpose(g_full.reshape(d, 4, d_in4), (1, 0, 2))
        split_cols = lambda t: jnp.transpose(t.reshape(BRANCH_W, 4, d // 4), (1, 0, 2))
        pieces = [g_in, split_cols(g_wb_dn), split_cols(g_wb_ret), split_cols(g_wb_pool), g_out.reshape(4, d // 4, d), g_gate, g_up, g_down]
        flat = jnp.concatenate([p.reshape(4, -1) for p in pieces], axis=1)
        rows = gathered[l].shape[1]
        big_grads[l] = jnp.pad(flat, ((0, 0), (0, rows * PACK_W - flat.shape[1]))).reshape(4, rows, PACK_W)
        small_grads[l] = [d_pre[0], d_post[0], d_convw, d_alog[0, :HEADS], d_dtb[0, :HEADS], d_dnw[0], d_retw.reshape(-1), d_poolw, d_pools[0],
                          d_fpre[0], d_fpost[0]]

    reduced = [_unpack(_reduce_scatter(big_grads[l]), big_shapes) for l in range(depth)]
    big_g = [jnp.stack([reduced[l][i] for l in range(depth)]) for i in range(len(big_shapes))]
    small_stacked = [jnp.stack([small_grads[l][i] for l in range(depth)]) for i in range(len(small_grads[0]))]
    sflat = jnp.concatenate([t.reshape(-1) for t in small_stacked])
    ssum = _all_reduce_small(_to_exchange_buffer(sflat)).reshape(-1)
    small_g, off = [], 0
    for t in small_stacked:
        small_g.append(ssum[off:off + t.size].reshape(t.shape))
        off += t.size
    (g_pre, g_post, g_conv_full, g_alog, g_dtb, g_dnw, g_retw, g_poolw, g_pools, g_fpre, g_fpost) = small_g
    g_conv = lax.dynamic_slice_in_dim(g_conv_full.reshape(depth, 4, 4, qkv_w // 4), xk, 1, axis=2)[:, :, 0, :]

    g_w_in, g_wbd, g_wbr, g_wbp, g_wo, g_fg, g_fu, g_fd = big_g
    large = [("w_in", w_in, g_w_in, m_w_in, v_w_in), ("w_branch_dn", w_branch_dn, g_wbd, m_w_branch_dn, v_w_branch_dn),
             ("w_branch_ret", w_branch_ret, g_wbr, m_w_branch_ret, v_w_branch_ret), ("w_branch_pool", w_branch_pool, g_wbp, m_w_branch_pool, v_w_branch_pool),
             ("w_out", w_out, g_wo, m_w_out, v_w_out), ("ffn_gate", ffn_gate, g_fg, m_ffn_gate, v_ffn_gate),
             ("ffn_up", ffn_up, g_fu, m_ffn_up, v_ffn_up), ("ffn_down", ffn_down, g_fd, m_ffn_down, v_ffn_down)]
    small = [("mix_pre_norm", mix_pre_norm, g_pre, m_mix_pre_norm, v_mix_pre_norm), ("mix_post_norm", mix_post_norm, g_post, m_mix_post_norm, v_mix_post_norm),
             ("dn_conv", dn_conv, g_conv, m_dn_conv, v_dn_conv), ("dn_A_log", dn_A_log, g_alog, m_dn_A_log, v_dn_A_log),
             ("dn_dt_bias", dn_dt_bias, g_dtb, m_dn_dt_bias, v_dn_dt_bias), ("dn_out_norm", dn_out_norm, g_dnw, m_dn_out_norm, v_dn_out_norm),
             ("ret_out_norm", ret_out_norm, g_retw, m_ret_out_norm, v_ret_out_norm), ("pool_w", pool_w, g_poolw, m_pool_w, v_pool_w),
             ("pool_scale", pool_scale, g_pools, m_pool_scale, v_pool_scale), ("ffn_pre_norm", ffn_pre_norm, g_fpre, m_ffn_pre_norm, v_ffn_pre_norm),
             ("ffn_post_norm", ffn_post_norm, g_fpost, m_ffn_post_norm, v_ffn_post_norm)]
    flat2 = lambda t: t.reshape(-1, t.shape[-1])
    sd, sm, sv_ = _adamw_small(*[[flat2(t[i]) for t in small] for i in (1, 2, 3, 4)])
    grads, deltas, new_m, new_v = {}, {}, {}, {}
    for name, w, g, m, v in large:
        grads[name] = g
        deltas[name], new_m[name], new_v[name] = _adamw_large("adamw_" + name, w, g, m, v)
    for i, (name, w, g, m, v) in enumerate(small):
        grads[name] = g.reshape(w.shape)
        deltas[name], new_m[name], new_v[name] = sd[i].reshape(w.shape), sm[i].reshape(w.shape), sv_[i].reshape(w.shape)
    order = ["mix_pre_norm", "mix_post_norm", "w_in", "dn_conv", "dn_A_log", "dn_dt_bias", "dn_out_norm", "ret_out_norm", "pool_w", "pool_scale",
             "w_branch_dn", "w_branch_ret", "w_branch_pool", "w_out", "ffn_pre_norm", "ffn_post_norm", "ffn_gate", "ffn_up", "ffn_down"]
    return (loss, gx[None], *[grads[n] for n in order], *[deltas[n] for n in order], *[new_m[n] for n in order], *[new_v[n] for n in order])
```

```python
import functools

import numpy as np
import jax
import jax.numpy as jnp
from jax import lax
from jax.experimental import pallas as pl
from jax.experimental.pallas import tpu as pltpu

F32 = jnp.float32
MXU_DTYPE = jnp.bfloat16
WIRE_DTYPE = jnp.bfloat16
HI = lax.Precision.HIGHEST
SDS = jax.ShapeDtypeStruct
MESH = pl.DeviceIdType.MESH

CHUNK = 64
HEADS = 4
HEAD_DIM = 128
BRANCH_W = HEADS * HEAD_DIM
TAIL_W = 128
GROUP_ROWS = 512
RMS_EPS = 1e-6
GN_EPS = 1e-5
ROPE_BASE = 10000.0
POOL_WINDOWS = (2, 4, 8, 16)
VMEM_LIMIT_BYTES = 56 * 1024 * 1024

ADAM_LR = 0.001
ADAM_B1 = 0.9
ADAM_B2 = 0.999
ADAM_EPS = 1e-08
ADAM_WD = 0.01
ADAM_STEP = 10


def _params(sem=None):
    return pltpu.CompilerParams(dimension_semantics=sem, vmem_limit_bytes=VMEM_LIMIT_BYTES)


def _mx(v):
    return v.astype(MXU_DTYPE)


def _mm(name, a, b, *, ta=False, tb=False, contract_batch=False, out_dtype=F32, tm=1024, tn=None, tk=None):
    a3 = a if a.ndim == 3 else a[None]
    b3 = b if b.ndim == 3 else b[None]
    ba, bb = a3.shape[0], b3.shape[0]
    nb = max(ba, bb)
    k_dim, m_dim = (a3.shape[1], a3.shape[2]) if ta else (a3.shape[2], a3.shape[1])
    n_dim = b3.shape[1] if tb else b3.shape[2]
    assert k_dim == (b3.shape[2] if tb else b3.shape[1])
    tm = min(tm, m_dim)
    tn = n_dim if tn is None else min(tn, n_dim)
    tk = k_dim if tk is None else min(tk, k_dim)
    assert m_dim % tm == 0 and n_dim % tn == 0 and k_dim % tk == 0, (name, m_dim, n_dim, k_dim, tm, tn, tk)
    nbo, nbk = (1, nb) if contract_batch else (nb, 1)
    nk = k_dim // tk
    grid = (nbo, m_dim // tm, n_dim // tn, nbk, nk)

    def bsel(has, bo, kb):
        return (kb if contract_batch else bo) if has > 1 else 0

    a_spec = pl.BlockSpec((1, tk, tm) if ta else (1, tm, tk),
                          lambda bo, i, j, kb, kk: (bsel(ba, bo, kb),) + ((kk, i) if ta else (i, kk)))
    b_spec = pl.BlockSpec((1, tn, tk) if tb else (1, tk, tn),
                          lambda bo, i, j, kb, kk: (bsel(bb, bo, kb),) + ((j, kk) if tb else (kk, j)))
    o_spec = pl.BlockSpec((1, tm, tn), lambda bo, i, j, kb, kk: (bo, i, j))
    dims = (((0 if ta else 1,), (1 if tb else 0,)), ((), ()))
    steps = nbk * nk

    def body(a_ref, b_ref, o_ref, acc_ref):
        part = lax.dot_general(_mx(a_ref[0]), _mx(b_ref[0]), dims, preferred_element_type=F32)
        if steps == 1:
            o_ref[0] = part.astype(o_ref.dtype)
        else:
            step = pl.program_id(3) * nk + pl.program_id(4)

            @pl.when(step == 0)
            def _():
                acc_ref[...] = part

            @pl.when(step > 0)
            def _():
                acc_ref[...] += part

            @pl.when(step == steps - 1)
            def _():
                o_ref[0] = acc_ref[...].astype(o_ref.dtype)

    out = pl.pallas_call(
        body, name=name, grid=grid, in_specs=[a_spec, b_spec], out_specs=o_spec,
        out_shape=SDS((nbo, m_dim, n_dim), out_dtype),
        scratch_shapes=[pltpu.VMEM((tm, tn) if steps > 1 else (8, 128), F32)],
        compiler_params=_params(("parallel", "parallel", "parallel", "arbitrary", "arbitrary")),
    )(a3, b3)
    return out if nbo > 1 else out[0]


def _rowwise(name, fn, rows, fulls, out_rows, out_accs=(), *, tile):
    s = rows[0][0].shape[-2]
    tile = min(tile, s)
    assert s % tile == 0
    n_rows, n_full, n_or = len(rows), len(fulls), len(out_rows)

    def rspec(lead, width, cb):
        nl = len(lead)
        return pl.BlockSpec(tuple(lead) + (tile, width), lambda i, nl=nl, cb=cb: (0,) * nl + (i, cb))

    def fspec(shape):
        nd = len(shape)
        return pl.BlockSpec(tuple(shape), lambda i, nd=nd: (0,) * nd)

    in_specs = [rspec(arr.shape[:-2], w, cb) for arr, w, cb in rows] + [fspec(f.shape) for f in fulls]
    out_specs = [rspec(lead, w, 0) for lead, w, _ in out_rows] + [fspec(shape) for shape, _ in out_accs]
    out_shape = [SDS(tuple(lead) + (s, w), dt) for lead, w, dt in out_rows] + [SDS(tuple(shape), dt) for shape, dt in out_accs]

    def body(*refs):
        ins = [r[...] for r in refs[:n_rows + n_full]]
        outs = fn(*ins)
        o_refs = refs[n_rows + n_full:]
        for o_ref, val in zip(o_refs[:n_or], outs[:n_or]):
            o_ref[...] = val.astype(o_ref.dtype)
        first = pl.program_id(0) == 0
        for a_ref, val in zip(o_refs[n_or:], outs[n_or:]):
            @pl.when(first)
            def _(a_ref=a_ref, val=val):
                a_ref[...] = val.astype(a_ref.dtype)

            @pl.when(jnp.logical_not(first))
            def _(a_ref=a_ref, val=val):
                a_ref[...] += val.astype(a_ref.dtype)

    return pl.pallas_call(
        body, name=name, grid=(s // tile,), in_specs=in_specs, out_specs=out_specs, out_shape=out_shape,
        compiler_params=_params(("arbitrary",)),
    )(*[r[0] for r in rows], *fulls)


def _rms(x, w):
    return x * lax.rsqrt(jnp.mean(x * x, axis=-1, keepdims=True) + RMS_EPS) * w


def _silu(x):
    return x * jax.nn.sigmoid(x)


def _softplus(x):
    return jnp.maximum(x, 0.0) + jnp.log(1.0 + jnp.exp(-jnp.abs(x)))


def _pick_lane(row, idx):
    lane = lax.broadcasted_iota(jnp.int32, row.shape, row.ndim - 1)
    return jnp.sum(jnp.where(lane == idx, row, 0.0), axis=-1, keepdims=True)


def _pick_row(mat, idx):
    r = lax.broadcasted_iota(jnp.int32, mat.shape, 0)
    return jnp.sum(jnp.where(r == idx, mat, 0.0), axis=0, keepdims=True)


def _shift_rows(x, s):
    n = x.shape[0]
    rolled = pltpu.roll(x, s % n, 0)
    row = lax.broadcasted_iota(jnp.int32, x.shape, 0)
    keep = row >= s if s > 0 else row < n + s
    return jnp.where(keep, rolled, 0.0)


def _make_shift(s):
    @jax.custom_vjp
    def shift(x):
        return _shift_rows(x, s)

    shift.defvjp(lambda x: (_shift_rows(x, s), None), lambda _, g: (_shift_rows(g, -s),))
    return shift


_SHIFT = {s: _make_shift(s) for s in (1, 2, 3, 4, 8)}


@jax.custom_vjp
def _swap_halves(x):
    return pltpu.roll(x, HEAD_DIM // 2, 1)


_swap_halves.defvjp(lambda x: (pltpu.roll(x, HEAD_DIM // 2, 1), None), lambda _, g: (pltpu.roll(g, HEAD_DIM // 2, 1),))


def _bdot(a, b, ca, cb, precision=None):
    return lax.dot_general(a, b, (((ca,), (cb,)), ((0,), (0,))), precision=precision, preferred_element_type=F32)


def _dot(a, b, ca=1, cb=0):
    return lax.dot_general(a, b, (((ca,), (cb,)), ((), ())), preferred_element_type=F32)


def _conv_math(x, w0, w1, w2, w3):
    y = x * w3 + _SHIFT[1](x) * w2 + _SHIFT[2](x) * w1 + _SHIFT[3](x) * w0
    return _silu(y)


def _conv_fwd(projm, convw, base_block):
    s = projm.shape[0]
    nblk = convw.shape[1] // HEAD_DIM

    def body(x_ref, w_ref, o_ref):
        o_ref[...] = _conv_math(x_ref[...], *[w_ref[k:k + 1, :] for k in range(4)])

    return pl.pallas_call(
        body, name="dn_conv_fwd", grid=(nblk,),
        in_specs=[pl.BlockSpec((s, HEAD_DIM), lambda j: (0, base_block + j)), pl.BlockSpec((4, HEAD_DIM), lambda j: (0, j))],
        out_specs=pl.BlockSpec((s, HEAD_DIM), lambda j: (0, j)),
        out_shape=SDS((s, convw.shape[1]), F32), compiler_params=_params(("parallel",)),
    )(projm, convw)


def _conv_bwd(projm, convw, dy, base_block):
    s = projm.shape[0]
    nblk = convw.shape[1] // HEAD_DIM

    def body(x_ref, w_ref, dy_ref, dx_ref, dw_ref):
        _, vjp = jax.vjp(_conv_math, x_ref[...], *[w_ref[k:k + 1, :] for k in range(4)])
        grads = vjp(dy_ref[...])
        dx_ref[...] = grads[0]
        for k in range(4):
            dw_ref[k:k + 1, :] = jnp.sum(grads[1 + k], axis=0, keepdims=True)

    return pl.pallas_call(
        body, name="dn_conv_bwd", grid=(nblk,),
        in_specs=[pl.BlockSpec((s, HEAD_DIM), lambda j: (0, base_block + j)), pl.BlockSpec((4, HEAD_DIM), lambda j: (0, j)),
                  pl.BlockSpec((s, HEAD_DIM), lambda j: (0, j))],
        out_specs=[pl.BlockSpec((s, HEAD_DIM), lambda j: (0, j)), pl.BlockSpec((4, HEAD_DIM), lambda j: (0, j))],
        out_shape=[SDS((s, convw.shape[1]), F32), SDS(convw.shape, F32)], compiler_params=_params(("parallel",)),
    )(projm, convw, dy)


def _pool_math(grp, u, w, scale):
    w2 = u + _SHIFT[1](u)
    w4 = w2 + _SHIFT[2](w2)
    w8 = w4 + _SHIFT[4](w4)
    w16 = w8 + _SHIFT[8](w8)
    t1 = (lax.broadcasted_iota(jnp.int32, (u.shape[0], 1), 0) + 1).astype(F32)
    pooled = w16 / jnp.minimum(t1, float(POOL_WINDOWS[3]))
    for gi, acc in ((2, w8), (1, w4), (0, w2)):
        pooled = jnp.where(grp == gi, acc / jnp.minimum(t1, float(POOL_WINDOWS[gi])), pooled)
    mixed = pooled - u
    return _dot(_mx(mixed), _mx(w)) * scale


def _pool_fwd(projm, pool_w, pool_scale, base_block):
    s = projm.shape[0]

    def body(u_ref, w_ref, s_ref, o_ref):
        o_ref[...] = _pool_math(pl.program_id(0), u_ref[...], w_ref[0], s_ref[...]).astype(o_ref.dtype)

    return pl.pallas_call(
        body, name="pool_fwd", grid=(HEADS,),
        in_specs=[pl.BlockSpec((s, HEAD_DIM), lambda j: (0, base_block + j)), pl.BlockSpec((1, HEAD_DIM, HEAD_DIM), lambda j: (j, 0, 0)),
                  pl.BlockSpec((1, HEAD_DIM), lambda j: (0, j))],
        out_specs=pl.BlockSpec((s, HEAD_DIM), lambda j: (0, j)),
        out_shape=SDS((s, BRANCH_W), MXU_DTYPE), compiler_params=_params(("parallel",)),
    )(projm, pool_w, pool_scale)


def _pool_bwd(projm, pool_w, pool_scale, do, base_block):
    s = projm.shape[0]

    def body(u_ref, w_ref, s_ref, do_ref, du_ref, dw_ref, ds_ref):
        _, vjp = jax.vjp(functools.partial(_pool_math, pl.program_id(0)), u_ref[...], w_ref[0], s_ref[...])
        du, dw, ds = vjp(do_ref[...])
        du_ref[...] = du
        dw_ref[0] = dw
        ds_ref[...] = ds

    return pl.pallas_call(
        body, name="pool_bwd", grid=(HEADS,),
        in_specs=[pl.BlockSpec((s, HEAD_DIM), lambda j: (0, base_block + j)), pl.BlockSpec((1, HEAD_DIM, HEAD_DIM), lambda j: (j, 0, 0)),
                  pl.BlockSpec((1, HEAD_DIM), lambda j: (0, j)), pl.BlockSpec((s, HEAD_DIM), lambda j: (0, j))],
        out_specs=[pl.BlockSpec((s, HEAD_DIM), lambda j: (0, j)), pl.BlockSpec((1, HEAD_DIM, HEAD_DIM), lambda j: (j, 0, 0)),
                   pl.BlockSpec((1, HEAD_DIM), lambda j: (0, j))],
        out_shape=[SDS((s, BRANCH_W), F32), SDS(pool_w.shape, F32), SDS(pool_scale.shape, F32)],
        compiler_params=_params(("parallel",)),
    )(projm, pool_w, pool_scale, do)


def _scan_specs(s, rows, reverse):
    ng = s // GROUP_ROWS

    def gi(i):
        return ng - 1 - i if reverse else i

    specs = []
    for _, base, per_head in rows:
        specs.append(pl.BlockSpec((GROUP_ROWS, HEAD_DIM), lambda i, h, base=base, ph=per_head: (gi(i), base + (h if ph else 0))))
    return specs, gi, ng


def _scan_fwd(name, group_fn, rows, params, s):
    specs, gi, ng = _scan_specs(s, rows, False)
    n_in = len(rows) + len(params)
    pspecs = [pl.BlockSpec(p.shape, lambda i, h, nd=p.ndim: (0,) * nd) for p in params]

    def body(*refs):
        ins = [r[...] for r in refs[:n_in]]
        o_ref, saved_ref, st_ref = refs[n_in:]
        g, h = pl.program_id(0), pl.program_id(1)

        @pl.when(g == 0)
        def _():
            st_ref[h] = jnp.zeros((HEAD_DIM, HEAD_DIM), F32)

        state = st_ref[h]
        saved_ref[0, 0] = state
        out, new_state = group_fn(h, *ins, state)
        o_ref[...] = out.astype(o_ref.dtype)
        st_ref[h] = new_state

    return pl.pallas_call(
        body, name=name, grid=(ng, HEADS), in_specs=specs + pspecs,
        out_specs=[pl.BlockSpec((GROUP_ROWS, HEAD_DIM), lambda i, h: (i, h)),
                   pl.BlockSpec((1, 1, HEAD_DIM, HEAD_DIM), lambda i, h: (i, h, 0, 0))],
        out_shape=[SDS((s, BRANCH_W), MXU_DTYPE), SDS((ng, HEADS, HEAD_DIM, HEAD_DIM), F32)],
        scratch_shapes=[pltpu.VMEM((HEADS, HEAD_DIM, HEAD_DIM), F32)],
        compiler_params=_params(("arbitrary", "arbitrary")),
    )(*[r[0] for r in rows], *params)


def _scan_bwd(name, group_fn, rows, n_diff, params, saved, do, s):
    specs, gi, ng = _scan_specs(s, rows, True)
    n_rows, n_par = len(rows), len(params)
    pspecs = [pl.BlockSpec(p.shape, lambda i, h, nd=p.ndim: (0,) * nd) for p in params]
    in_specs = specs + pspecs + [pl.BlockSpec((1, 1, HEAD_DIM, HEAD_DIM), lambda i, h: (gi(i), h, 0, 0)),
                                 pl.BlockSpec((GROUP_ROWS, HEAD_DIM), lambda i, h: (gi(i), h))]
    out_specs, out_shape = [], []
    for _, _, per_head in rows[:n_diff]:
        out_specs.append(pl.BlockSpec((GROUP_ROWS, HEAD_DIM), lambda i, h, ph=per_head: (gi(i), h if ph else 0)))
        out_shape.append(SDS((s, BRANCH_W if per_head else HEAD_DIM), F32))
    for p in params:
        out_specs.append(pl.BlockSpec(p.shape, lambda i, h, nd=p.ndim: (0,) * nd))
        out_shape.append(SDS(p.shape, F32))

    def body(*refs):
        row_vals = [r[...] for r in refs[:n_rows]]
        par_vals = [r[...] for r in refs[n_rows:n_rows + n_par]]
        saved_ref, do_ref = refs[n_rows + n_par:n_rows + n_par + 2]
        outs = refs[n_rows + n_par + 2:-1]
        dst_ref = refs[-1]
        i, h = pl.program_id(0), pl.program_id(1)

        @pl.when(i == 0)
        def _():
            dst_ref[h] = jnp.zeros((HEAD_DIM, HEAD_DIM), F32)

        consts = row_vals[n_diff:]

        def f(*args):
            return group_fn(h, *args[:n_diff], *consts, *args[n_diff:])

        _, vjp = jax.vjp(f, *row_vals[:n_diff], *par_vals, saved_ref[0, 0])
        grads = vjp((do_ref[...], dst_ref[h]))
        dst_ref[h] = grads[-1]
        for k, (_, _, per_head) in enumerate(rows[:n_diff]):
            if per_head:
                outs[k][...] = grads[k]
            else:
                @pl.when(h == 0)
                def _(k=k):
                    outs[k][...] = grads[k]

                @pl.when(h > 0)
                def _(k=k):
                    outs[k][...] += grads[k]
        first = jnp.logical_and(i == 0, h == 0)
        for k in range(n_par):
            @pl.when(first)
            def _(k=k):
                outs[n_diff + k][...] = grads[n_diff + k]

            @pl.when(jnp.logical_not(first))
            def _(k=k):
                outs[n_diff + k][...] += grads[n_diff + k]

    return pl.pallas_call(
        body, name=name, grid=(ng, HEADS), in_specs=in_specs, out_specs=out_specs, out_shape=out_shape,
        scratch_shapes=[pltpu.VMEM((HEADS, HEAD_DIM, HEAD_DIM), F32)],
        compiler_params=_params(("arbitrary", "arbitrary")),
    )(*[r[0] for r in rows], *params, saved, do)


def _dn_group(h, qc, kc, vc, z, tail, alog, dtb, dnw, state):
    c = CHUNK
    nb = qc.shape[0] // c
    beta = jax.nn.sigmoid(_pick_lane(tail, h))
    g = -jnp.exp(_pick_lane(alog, h)) * _softplus(_pick_lane(tail, HEADS + h) + _pick_lane(dtb, h))
    q = qc * lax.rsqrt(jnp.sum(qc * qc, axis=-1, keepdims=True) + 1e-6) * (HEAD_DIM ** -0.5)
    k = kc * lax.rsqrt(jnp.sum(kc * kc, axis=-1, keepdims=True) + 1e-6)
    q, k, v = (t.reshape(nb, c, HEAD_DIM) for t in (q, k, vc))
    beta = beta.reshape(nb, c, 1)
    g = g.reshape(nb, c, 1)
    g_wide = jnp.broadcast_to(g, (nb, c, HEAD_DIM))
    g_sq = jnp.broadcast_to(g, (nb, c, c))
    ri = lax.broadcasted_iota(jnp.int32, (nb, c, c), 1)
    ci = lax.broadcasted_iota(jnp.int32, (nb, c, c), 2)
    causal = ri >= ci
    lower = causal.astype(F32)
    upper = (ri <= ci).astype(F32)
    cum_wide = _bdot(lower, g_wide, 2, 1, HI)
    cum_i = _bdot(lower, g_sq, 2, 1, HI)
    cum_j = _bdot(g_sq, upper, 1, 1, HI)
    decay = jnp.where(causal, jnp.exp(jnp.where(causal, cum_i - cum_j, 0.0)), 0.0)
    k_beta = k * beta
    m = jnp.where(ri > ci, _bdot(k_beta, k, 2, 2, HI) * decay, 0.0)
    p = -m
    inv = jnp.where(ri == ci, 1.0, 0.0) + p
    for _ in range(5):
        p = _bdot(p, p, 2, 1, HI)
        inv = inv + _bdot(inv, p, 2, 1, HI)
    e_cum = jnp.exp(cum_wide)
    u = _bdot(inv, v * beta, 2, 1, HI)
    w = _bdot(inv, k_beta * e_cum, 2, 1, HI)
    attn = jnp.where(causal, _bdot(q, k, 2, 2, HI) * decay, 0.0)
    q_dec = q * e_cum
    g_last = jnp.sum(g_wide, axis=1, keepdims=True)
    k_tail = k * jnp.exp(g_last - cum_wide)
    d_last = jnp.exp(g_last)
    outs = []
    for n in range(nb):
        st = _mx(state)
        v_new = u[n] - _dot(_mx(w[n]), st)
        outs.append(_dot(_mx(q_dec[n]), st) + _dot(_mx(attn[n]), _mx(v_new)))
        state = state * d_last[n] + _dot(_mx(k_tail[n]), _mx(v_new), 0, 0)
    o = jnp.concatenate(outs, axis=0)
    o = o * lax.rsqrt(jnp.mean(o * o, axis=-1, keepdims=True) + RMS_EPS) * dnw * _silu(z)
    return o, state


def _ret_group(h, q, k, v, gate, cosf, sinf, retw, lgam, state):
    c = CHUNK
    nb = q.shape[0] // c
    lg = _pick_lane(lgam, h)
    rq = q * cosf + _swap_halves(q) * sinf
    rk = (k * cosf + _swap_halves(k) * sinf) * (HEAD_DIM ** -0.5)
    rq, rk, v3 = (t.reshape(nb, c, HEAD_DIM) for t in (rq, rk, v))
    ri = lax.broadcasted_iota(jnp.int32, (nb, c, c), 1)
    ci = lax.broadcasted_iota(jnp.int32, (nb, c, c), 2)
    dmask = jnp.exp(jnp.abs(ri - ci).astype(F32) * lg)
    scores = _bdot(_mx(rq), _mx(rk), 2, 2) * dmask
    o_inner = _bdot(_mx(scores), _mx(v3), 2, 1)
    pos = lax.broadcasted_iota(jnp.int32, (nb, c, 1), 1).astype(F32)
    xi = jnp.exp((pos + 1.0) * lg)
    zeta = jnp.exp((c - 1.0 - pos) * lg)
    kv = _bdot(_mx(rk * zeta), _mx(v3), 1, 1)
    chunk_decay = jnp.exp(float(c) * lg)
    entering = []
    for n in range(nb):
        entering.append(state)
        state = state * chunk_decay + kv[n]
    r_prev = jnp.stack(entering, axis=0)
    o = (o_inner + _bdot(_mx(rq * xi), _mx(r_prev), 2, 1)).reshape(nb * c, HEAD_DIM)
    mu = jnp.mean(o, axis=-1, keepdims=True)
    var = jnp.mean(jnp.square(o - mu), axis=-1, keepdims=True)
    o = (o - mu) * lax.rsqrt(var + GN_EPS) * _pick_row(retw, h) * _silu(gate)
    return o, state


def _place():
    x, y, c = lax.axis_index("x"), lax.axis_index("y"), lax.axis_index("c")
    chips = [(x, 1 - y), (1 - x, y), (1 - x, 1 - y)]
    return x, y, c, chips


_HBM = pl.BlockSpec(memory_space=pltpu.HBM)


LOCAL_COPY_CHUNKS = 4


def _half(c, rows, align):
    hr = rows // 2
    return pl.ds(pl.multiple_of(c * hr, align), hr), pl.ds(pl.multiple_of((1 - c) * hr, align), hr)


def _gather_weights(shards):
    n = len(shards)

    def body(*refs):
        x_refs, o_refs = refs[:n], refs[n:2 * n]
        send_sems, recv_sems, local_sems = refs[2 * n:]
        x, y, c, chips = _place()
        sib = (x, y, 1 - c)
        me = 2 * x + y
        slots = [2 * chip[0] + chip[1] for chip in chips]

        def copy(i, sem, src, slot, rows, to):
            return pltpu.make_async_remote_copy(src_ref=src, dst_ref=o_refs[i].at[slot, rows], send_sem=send_sems.at[6 * i + sem],
                                                recv_sem=recv_sems.at[6 * i + sem], device_id=to, device_id_type=MESH)

        halves = [_half(c, x_refs[i].shape[0], 16) for i in range(n)]
        own = []
        for i in range(n):
            step = x_refs[i].shape[0] // LOCAL_COPY_CHUNKS
            for q in range(LOCAL_COPY_CHUNKS):
                rows = pl.ds(q * step, step)
                own.append(pltpu.make_async_copy(x_refs[i].at[rows], o_refs[i].at[me, rows], local_sems.at[LOCAL_COPY_CHUNKS * i + q]))
        first = [copy(i, j, x_refs[i].at[halves[i][0]], me, halves[i][0], (*chip, c)) for i in range(n) for j, chip in enumerate(chips)]
        for cp in first + own:
            cp.start()
        passed = []
        for j, chip in enumerate(chips):
            for i in range(n):
                copy(i, j, x_refs[i].at[halves[i][0]], slots[j], halves[i][0], (*chip, c)).wait_recv()
                fwd = copy(i, 3 + j, o_refs[i].at[slots[j], halves[i][0]], slots[j], halves[i][0], sib)
                fwd.start()
                passed.append(fwd)
        for j in range(3):
            for i in range(n):
                copy(i, 3 + j, x_refs[i].at[halves[i][0]], slots[j], halves[i][1], sib).wait_recv()
        for cp in first + passed:
            cp.wait_send()
        for cp in own:
            cp.wait()

    return pl.pallas_call(
        body, name="gather_weights", out_shape=[SDS((4,) + t.shape, t.dtype) for t in shards], in_specs=[_HBM] * n, out_specs=[_HBM] * n,
        scratch_shapes=[pltpu.SemaphoreType.DMA((6 * n,)), pltpu.SemaphoreType.DMA((6 * n,)), pltpu.SemaphoreType.DMA((LOCAL_COPY_CHUNKS * n,))],
    )(*shards)


def _swap_with_sibling(grads):
    n = len(grads)

    def body(*refs):
        g_refs, o_refs = refs[:n], refs[n:2 * n]
        send_sems, recv_sems = refs[2 * n:]
        x, y, c, _ = _place()
        copies = [pltpu.make_async_remote_copy(src_ref=g_refs[i].at[:, _half(c, g_refs[i].shape[1], 16)[1]], dst_ref=o_refs[i],
                                               send_sem=send_sems.at[i], recv_sem=recv_sems.at[i], device_id=(x, y, 1 - c), device_id_type=MESH)
                  for i in range(n)]
        for cp in copies:
            cp.start()
        for cp in copies:
            cp.wait()

    return pl.pallas_call(
        body, name="grad_to_sibling", out_shape=[SDS((4, t.shape[1] // 2, t.shape[2]), t.dtype) for t in grads],
        in_specs=[_HBM] * n, out_specs=[_HBM] * n, scratch_shapes=[pltpu.SemaphoreType.DMA((n,)), pltpu.SemaphoreType.DMA((n,))],
    )(*grads)


def _scatter_to_chips(parts):
    n = len(parts)

    def body(*refs):
        p_refs, o_refs = refs[:n], refs[n:2 * n]
        send_sems, recv_sems = refs[2 * n:]
        x, y, c, chips = _place()
        copies = [pltpu.make_async_remote_copy(src_ref=p_refs[i].at[2 * chip[0] + chip[1]], dst_ref=o_refs[i].at[j], send_sem=send_sems.at[3 * i + j],
                                               recv_sem=recv_sems.at[3 * i + j], device_id=(*chip, c), device_id_type=MESH)
                  for j, chip in enumerate(chips) for i in range(n)]
        for cp in copies:
            cp.start()
        for cp in copies:
            cp.wait()

    return pl.pallas_call(
        body, name="grad_to_chips", out_shape=[SDS((3,) + t.shape[1:], t.dtype) for t in parts], in_specs=[_HBM] * n, out_specs=[_HBM] * n,
        scratch_shapes=[pltpu.SemaphoreType.DMA((3 * n,)), pltpu.SemaphoreType.DMA((3 * n,))],
    )(*parts)


def _join_halves(bufs):
    n = len(bufs)

    def body(*refs):
        o_refs = refs[n:2 * n]
        send_sems, recv_sems = refs[2 * n:]
        x, y, c, _ = _place()
        sends, recvs = [], []
        for i in range(n):
            mine, theirs = _half(c, o_refs[i].shape[0], 8)
            sends.append(pltpu.make_async_remote_copy(src_ref=o_refs[i].at[mine], dst_ref=o_refs[i].at[mine], send_sem=send_sems.at[i],
                                                      recv_sem=recv_sems.at[i], device_id=(x, y, 1 - c), device_id_type=MESH))
            recvs.append(pltpu.make_async_remote_copy(src_ref=o_refs[i].at[mine], dst_ref=o_refs[i].at[theirs], send_sem=send_sems.at[i],
                                                      recv_sem=recv_sems.at[i], device_id=(x, y, 1 - c), device_id_type=MESH))
        for cp in sends:
            cp.start()
        for cp in recvs:
            cp.wait_recv()
        for cp in sends:
            cp.wait_send()

    return pl.pallas_call(
        body, name="grad_join_halves", out_shape=[SDS(t.shape, t.dtype) for t in bufs], in_specs=[_HBM] * n, out_specs=[_HBM] * n,
        input_output_aliases={i: i for i in range(n)},
        scratch_shapes=[pltpu.SemaphoreType.DMA((n,)), pltpu.SemaphoreType.DMA((n,))],
    )(*bufs)


def _add_sibling(name, grads, from_sib, place, n_tiles):
    n = len(grads)

    def body(place_ref, *refs):
        for i in range(n):
            refs[2 * n + i][...] = (refs[i][...].astype(F32) + refs[n + i][...].astype(F32)).astype(refs[2 * n + i].dtype)

    blocks = [(4, t.shape[1] // n_tiles, t.shape[2]) for t in from_sib]
    return pl.pallas_call(
        body, name=name,
        grid_spec=pltpu.PrefetchScalarGridSpec(
            num_scalar_prefetch=1, grid=(n_tiles,),
            in_specs=[pl.BlockSpec(b, lambda t, p: (0, p[0] * n_tiles + t, 0)) for b in blocks] + [pl.BlockSpec(b, lambda t, p: (0, t, 0)) for b in blocks],
            out_specs=[pl.BlockSpec(b, lambda t, p: (0, t, 0)) for b in blocks]),
        out_shape=[SDS(t.shape, t.dtype) for t in from_sib], compiler_params=_params(("parallel",)),
    )(place, *grads, *from_sib)


def _add_chips(name, parts, from_chips, place, n_tiles):
    n = len(parts)

    def body(place_ref, *refs):
        for i in range(n):
            p_ref, r_ref = refs[i], refs[n + i]
            refs[2 * n + i][...] = ((p_ref[0].astype(F32) + r_ref[0].astype(F32)) + r_ref[1].astype(F32)) + r_ref[2].astype(F32)

    tiles = [(t.shape[1] // n_tiles, t.shape[2]) for t in parts]
    return pl.pallas_call(
        body, name=name,
        grid_spec=pltpu.PrefetchScalarGridSpec(
            num_scalar_prefetch=1, grid=(n_tiles,),
            in_specs=[pl.BlockSpec((1,) + b, lambda t, p: (p[1], t, 0)) for b in tiles] + [pl.BlockSpec((3,) + b, lambda t, p: (0, t, 0)) for b in tiles],
            out_specs=[pl.BlockSpec(b, lambda t, p: (p[0] * n_tiles + t, 0)) for b in tiles]),
        out_shape=[SDS((2 * t.shape[1], t.shape[2]), F32) for t in parts], compiler_params=_params(("parallel",)),
    )(place, *parts, *from_chips)


def _reduce_scatter(grads):
    place = jnp.stack([lax.axis_index("c"), 2 * lax.axis_index("x") + lax.axis_index("y")]).astype(jnp.int32)
    from_sib = _swap_with_sibling(grads)
    part = _add_sibling("grad_add_sibling_wide", grads[:1], from_sib[:1], place, 4) + _add_sibling("grad_add_sibling", grads[1:], from_sib[1:], place, 2)
    from_chips = _scatter_to_chips(part)
    halves = _add_chips("grad_add_chips_wide", part[:1], from_chips[:1], place, 4) + _add_chips("grad_add_chips", part[1:], from_chips[1:], place, 2)
    return _join_halves(halves)


def _all_reduce_small(x):
    _, n, wd = x.shape

    def body(x_ref, o_ref, buf, send1, recv1, send2, recv2):
        mx, my, mc = lax.axis_index("x"), lax.axis_index("y"), lax.axis_index("c")
        me = 4 * mx + 2 * my + mc
        peers = []
        for rel in range(1, 8):
            px, py, pc = (1 - mx if rel & 4 else mx), (1 - my if rel & 2 else my), (1 - mc if rel & 1 else mc)
            peers.append(((px, py, pc), 4 * px + 2 * py + pc))
        out = [pltpu.make_async_remote_copy(src_ref=x_ref.at[pidx], dst_ref=buf.at[j], send_sem=send1.at[j], recv_sem=recv1.at[j],
                                            device_id=peer, device_id_type=MESH) for j, (peer, pidx) in enumerate(peers)]
        for cp in out:
            cp.start()
        for cp in out:
            cp.wait()
        acc = x_ref[me]
        for j in range(7):
            acc = acc + buf[j]
        o_ref[me] = acc
        back = [pltpu.make_async_remote_copy(src_ref=o_ref.at[me], dst_ref=o_ref.at[me], send_sem=send2.at[j], recv_sem=recv2.at[j],
                                             device_id=peer, device_id_type=MESH) for j, (peer, _) in enumerate(peers)]
        for cp in back:
            cp.start()
        for j, (peer, pidx) in enumerate(peers):
            pltpu.make_async_remote_copy(src_ref=o_ref.at[me], dst_ref=o_ref.at[pidx], send_sem=send2.at[j], recv_sem=recv2.at[j],
                                         device_id=peer, device_id_type=MESH).wait_recv()
        for cp in back:
            cp.wait_send()

    vm = pl.BlockSpec(memory_space=pltpu.VMEM)
    return pl.pallas_call(
        body, name="all_reduce_small", out_shape=SDS(x.shape, F32), in_specs=[vm], out_specs=vm,
        scratch_shapes=[pltpu.VMEM((7, n, wd), F32)] + [pltpu.SemaphoreType.DMA((7,))] * 4,
        compiler_params=pltpu.CompilerParams(vmem_limit_bytes=VMEM_LIMIT_BYTES),
    )(x)


def _rows128(t):
    if t.size % 128 == 0:
        return t.reshape(-1, 128)
    assert t.size < 128
    return jnp.pad(t.reshape(1, -1), ((0, 0), (0, 128 - t.size)))


def _all_reduce_arrays(arrays):
    rows = [_rows128(t) for t in arrays]
    total = sum(r.shape[0] for r in rows)
    padded = -(-total // 64) * 64
    buf = jnp.concatenate(rows + [jnp.zeros((padded - total, 128), F32)], axis=0).reshape(8, padded // 8, 128)
    summed = _all_reduce_small(buf).reshape(padded, 128)
    out, off = [], 0
    for t, r in zip(arrays, rows):
        blk = summed[off:off + r.shape[0]]
        out.append(blk.reshape(t.shape) if t.size % 128 == 0 else blk[0, :t.size].reshape(t.shape))
        off += r.shape[0]
    return out


def _column_slices(sources, lo, hi):
    out = []
    for arr, a, b in sources:
        s, e = max(lo, a), min(hi, b)
        if s < e:
            out.append(arr[:, s - a:e - a])
    return out


def _adamw_math(w, g, m, v):
    m = ADAM_B1 * m + (1.0 - ADAM_B1) * g
    v = ADAM_B2 * v + (1.0 - ADAM_B2) * jnp.square(g)
    m_hat = m / (1.0 - ADAM_B1 ** ADAM_STEP)
    v_hat = v / (1.0 - ADAM_B2 ** ADAM_STEP)
    delta = -ADAM_LR * (m_hat / (jnp.sqrt(v_hat) + ADAM_EPS) + ADAM_WD * w)
    return delta, m, v


def _adamw_large(name, w, g, m, v):
    shape = w.shape
    cols = shape[-1]
    rows = int(np.prod(shape[:-1]))
    tile = 256 if rows % 256 == 0 else rows
    args = [t.reshape(rows, cols) for t in (w, g, m, v)]
    outs = _rowwise(name, _adamw_math, [(a, cols, 0) for a in args], [], [((), cols, F32)] * 3, tile=tile)
    return [o.reshape(shape) for o in outs]


def _adamw_small(ws, gs, ms, vs):
    n = len(ws)

    def body(*refs):
        for i in range(n):
            d, m, v = _adamw_math(refs[i][...], refs[n + i][...], refs[2 * n + i][...], refs[3 * n + i][...])
            refs[4 * n + i][...] = d
            refs[5 * n + i][...] = m
            refs[6 * n + i][...] = v

    outs = pl.pallas_call(body, name="adamw_small", out_shape=[SDS(w.shape, F32) for w in ws] * 3,
                          compiler_params=pltpu.CompilerParams(vmem_limit_bytes=VMEM_LIMIT_BYTES))(*ws, *gs, *ms, *vs)
    return outs[:n], outs[n:2 * n], outs[2 * n:]


def _rotary_tables(s):
    half = HEAD_DIM // 2
    inv = ROPE_BASE ** (-jnp.arange(half, dtype=F32) / half)
    ang = jnp.arange(s, dtype=F32)[:, None] * inv[None, :]
    cos, sin = jnp.cos(ang), jnp.sin(ang)
    return jnp.concatenate([cos, cos], axis=-1), jnp.concatenate([-sin, sin], axis=-1)


def kernel(x, mix_pre_norm, mix_post_norm, w_in, dn_conv, dn_A_log, dn_dt_bias, dn_out_norm, ret_out_norm, pool_w, pool_scale, w_branch_dn, w_branch_ret, w_branch_pool, w_out, ffn_pre_norm, ffn_post_norm, ffn_gate, ffn_up, ffn_down, loss_target, m_mix_pre_norm, m_mix_post_norm, m_w_in, m_dn_conv, m_dn_A_log, m_dn_dt_bias, m_dn_out_norm, m_ret_out_norm, m_pool_w, m_pool_scale, m_w_branch_dn, m_w_branch_ret, m_w_branch_pool, m_w_out, m_ffn_pre_norm, m_ffn_post_norm, m_ffn_gate, m_ffn_up, m_ffn_down, v_mix_pre_norm, v_mix_post_norm, v_w_in, v_dn_conv, v_dn_A_log, v_dn_dt_bias, v_dn_out_norm, v_ret_out_norm, v_pool_w, v_pool_scale, v_w_branch_dn, v_w_branch_ret, v_w_branch_pool, v_w_out, v_ffn_pre_norm, v_ffn_post_norm, v_ffn_gate, v_ffn_up, v_ffn_down):
    depth = w_in.shape[0]
    s, d = x.shape[1], x.shape[2]
    d_in4 = w_in.shape[2]
    d_in = 4 * d_in4
    f4 = ffn_gate.shape[2]
    qkv_w = 3 * BRANCH_W
    g0 = 3 * d
    b_qkv, b_z = g0 // HEAD_DIM, (g0 + qkv_w) // HEAD_DIM
    b_rq, b_rk, b_rv, b_rg, b_pu = ((g0 + qkv_w + BRANCH_W * i) // HEAD_DIM for i in range(1, 6))
    main_w = g0 + qkv_w + 6 * BRANCH_W
    src_tail = qkv_w + BRANCH_W
    assert d_in == main_w + 2 * HEADS
    col_tile = 1536 if main_w % 1536 == 0 else 128
    tile = 256
    xk = lax.axis_index("x") * 2 + lax.axis_index("y")

    cosf, sinf = _rotary_tables(s)
    lgam = jnp.zeros((1, HEAD_DIM), F32).at[0, :HEADS].set(jnp.log1p(-jnp.exp2(-5.0 - jnp.arange(HEADS, dtype=F32))))

    def pad_lanes(vec):
        return jnp.zeros((1, HEAD_DIM), F32).at[0, :vec.shape[0]].set(vec)

    gathered = [_gather_weights([t[l].astype(WIRE_DTYPE) for t in (w_in, w_branch_dn, w_branch_ret, w_branch_pool, w_out, ffn_gate, ffn_up, ffn_down)])
                for l in range(depth)]
    conv_place = lax.dynamic_update_slice(jnp.zeros((4,) + dn_conv.shape, F32), dn_conv[None], (xk, 0, 0, 0))
    conv_sum = _all_reduce_arrays([conv_place * (1 - lax.axis_index("c")).astype(F32)])[0]
    conv_all = jnp.concatenate([conv_sum[k] for k in range(4)], axis=-1)
    rest = src_tail + 2 * HEADS

    def layer_weights(l):
        wi, wbd, wbr, wbp, wo, wg, wu, wdn = gathered[l]
        pieces = [(wi[k], k * d_in4, (k + 1) * d_in4) for k in range(4)]
        w_main = jnp.concatenate(_column_slices(pieces, rest + 5 * BRANCH_W, d_in) + _column_slices(pieces, 0, src_tail)
                                 + _column_slices(pieces, rest, rest + 5 * BRANCH_W), axis=1)
        w_tail = jnp.concatenate(_column_slices(pieces, src_tail, rest) + [jnp.zeros((d, TAIL_W - 2 * HEADS), wi.dtype)], axis=1)
        wb = [jnp.concatenate([t[k] for k in range(4)], axis=1) for t in (wbd, wbp, wbr)]
        return dict(main=w_main, tail=w_tail, wb=wb, out=wo.reshape(d, d), gate=wg, up=wu, down=wdn)

    def norm_fwd(name, xin, w):
        return _rowwise(name, lambda xv, wv: (_rms(xv, wv),), [(xin, d, 0)], [w], [((), d, MXU_DTYPE)], tile=tile)[0]

    def resnorm_fwd(name, xin, t, w):
        return _rowwise(name, lambda xv, tv, wv: (xv + _rms(tv, wv),), [(xin, d, 0), (t, d, 0)], [w], [((), d, F32)], tile=tile)[0]

    def merge_math(gates, p_dn, p_pool, p_ret):
        gt = jax.nn.sigmoid(gates)
        return gt[:, :d] * p_dn + gt[:, d:2 * d] * p_pool + gt[:, 2 * d:] * p_ret

    def swiglu_math(u, v):
        return _silu(u) * v

    xs, saved = [x[0]], []
    for l in range(depth):
        wts = layer_weights(l)
        xin = xs[-1]
        h = norm_fwd("mix_pre_norm_fwd", xin, mix_pre_norm[l][None])
        projm = _mm("proj_main", h, wts["main"], tn=col_tile)
        projt = _mm("proj_tail", h, wts["tail"])
        convw = conv_all[l]
        qkv_c = _conv_fwd(projm, convw, b_qkv)
        dn_par = [pad_lanes(dn_A_log[l]), pad_lanes(dn_dt_bias[l]), dn_out_norm[l][None]]
        dn_rows = [(qkv_c, 0, True), (qkv_c, HEADS, True), (qkv_c, 2 * HEADS, True), (projm, b_z, True), (projt, 0, False)]
        o_dn, dn_states = _scan_fwd("dn_fwd", _dn_group, dn_rows, dn_par, s)
        ret_par = [ret_out_norm[l].reshape(HEADS, HEAD_DIM), lgam]
        ret_rows = [(projm, b_rq, True), (projm, b_rk, True), (projm, b_rv, True), (projm, b_rg, True), (cosf, 0, False), (sinf, 0, False)]
        o_ret, ret_states = _scan_fwd("ret_fwd", _ret_group, ret_rows, ret_par, s)
        o_pool = _pool_fwd(projm, pool_w[l], pool_scale[l][None], b_pu)
        p_dn = _mm("branch_dn", o_dn, wts["wb"][0])
        p_pool = _mm("branch_pool", o_pool, wts["wb"][1])
        p_ret = _mm("branch_ret", o_ret, wts["wb"][2])
        ymix = _rowwise("merge_fwd", lambda *a: (merge_math(*a),), [(projm, 3 * d, 0), (p_dn, d, 0), (p_pool, d, 0), (p_ret, d, 0)], [],
                        [((), d, MXU_DTYPE)], tile=tile)[0]
        mo = _mm("mix_out", ymix, wts["out"])
        xmid = resnorm_fwd("mix_post_norm_fwd", xin, mo, mix_post_norm[l][None])
        h2 = norm_fwd("ffn_pre_norm_fwd", xmid, ffn_pre_norm[l][None])
        fu = _mm("ffn_gate", h2, wts["gate"])
        fv = _mm("ffn_up", h2, wts["up"])
        act = _rowwise("swiglu_fwd", lambda u, v: (swiglu_math(u, v),), [(fu, f4, 0), (fv, f4, 0)], [], [((4,), f4, MXU_DTYPE)], tile=tile)[0]
        fo = _mm("ffn_down", act, wts["down"], contract_batch=True)
        xout = resnorm_fwd("ffn_post_norm_fwd", xmid, fo, ffn_post_norm[l][None])
        xs.append(xout)
        saved.append(dict(wts=wts, h=h, projm=projm, convw=convw, dn_rows=dn_rows, dn_par=dn_par, dn_states=dn_states,
                          ret_rows=ret_rows, ret_par=ret_par, ret_states=ret_states, o_dn=o_dn, o_pool=o_pool, o_ret=o_ret,
                          p=(p_dn, p_pool, p_ret), ymix=ymix, mo=mo, xmid=xmid, h2=h2, fu=fu, fv=fv, act=act, fo=fo, xin=xin))

    def loss_math(y, t):
        diff = y - t
        part = 0.5 * jnp.sum(jnp.sum(diff * diff, axis=-1, keepdims=True) / d, axis=0, keepdims=True)
        return diff / d, jnp.broadcast_to(part, (8, 128))

    gx, loss_part = _rowwise("loss_head", loss_math, [(xs[-1], d, 0), (loss_target[0], d, 0)], [], [((), d, F32)], [((8, 128), F32)], tile=tile)
    loss = lax.psum(loss_part[0, 0], ("x", "y", "c"))

    def resnorm_bwd(name, t, gout, w):
        def fn(tv, gv, wv):
            _, vjp = jax.vjp(_rms, tv, wv)
            return vjp(gv)
        return _rowwise(name, fn, [(t, d, 0), (gout, d, 0)], [w], [((), d, F32)], [((1, d), F32)], tile=tile)

    def norm_bwd(name, xin, dh_a, dh_b, gout, w):
        def fn(xv, da, db, gv, wv):
            _, vjp = jax.vjp(_rms, xv, wv)
            dx_, dw_ = vjp(da + db)
            return gv + dx_, dw_
        return _rowwise(name, fn, [(xin, d, 0), (dh_a, d, 0), (dh_b, d, 0), (gout, d, 0)], [w], [((), d, F32)], [((1, d), F32)], tile=tile)

    def swiglu_bwd(u, v, da):
        _, vjp = jax.vjp(swiglu_math, u, v)
        return vjp(da)

    def merge_bwd(gates, p_dn, p_pool, p_ret, dyv):
        _, vjp = jax.vjp(merge_math, gates, p_dn, p_pool, p_ret)
        return vjp(dyv)

    big_grads, small_grads = [None] * depth, [None] * depth
    for l in reversed(range(depth)):
        sv = saved[l]
        wts = sv["wts"]
        dfo, d_fpost = resnorm_bwd("ffn_post_norm_bwd", sv["fo"], gx, ffn_post_norm[l][None])
        dact = _mm("ffn_down_dx", dfo, wts["down"], tb=True)
        g_down = _mm("ffn_down_dw", sv["act"], dfo, ta=True, out_dtype=WIRE_DTYPE, tm=f4)
        du, dv = _rowwise("swiglu_bwd", swiglu_bwd, [(sv["fu"], f4, 0), (sv["fv"], f4, 0), (dact, f4, 0)], [],
                          [((4,), f4, MXU_DTYPE), ((4,), f4, MXU_DTYPE)], tile=tile)
        g_gate = _mm("ffn_gate_dw", sv["h2"], du, ta=True, out_dtype=WIRE_DTYPE)
        g_up = _mm("ffn_up_dw", sv["h2"], dv, ta=True, out_dtype=WIRE_DTYPE)
        dh2_a = _mm("ffn_gate_dx", du, wts["gate"], tb=True, contract_batch=True)
        dh2_b = _mm("ffn_up_dx", dv, wts["up"], tb=True, contract_batch=True)
        gmid, d_fpre = norm_bwd("ffn_pre_norm_bwd", sv["xmid"], dh2_a, dh2_b, gx, ffn_pre_norm[l][None])
        dmo, d_post = resnorm_bwd("mix_post_norm_bwd", sv["mo"], gmid, mix_post_norm[l][None])
        dy = _mm("mix_out_dx", dmo, wts["out"], tb=True)
        g_out = _mm("mix_out_dw", sv["ymix"], dmo, ta=True, out_dtype=WIRE_DTYPE)
        dgates, dp_dn, dp_pool, dp_ret = _rowwise(
            "merge_bwd", merge_bwd, [(sv["projm"], 3 * d, 0), (sv["p"][0], d, 0), (sv["p"][1], d, 0), (sv["p"][2], d, 0), (dy, d, 0)], [],
            [((), 3 * d, MXU_DTYPE), ((), d, MXU_DTYPE), ((), d, MXU_DTYPE), ((), d, MXU_DTYPE)], tile=tile)
        do_dn = _mm("branch_dn_dx", dp_dn, wts["wb"][0], tb=True)
        do_pool = _mm("branch_pool_dx", dp_pool, wts["wb"][1], tb=True)
        do_ret = _mm("branch_ret_dx", dp_ret, wts["wb"][2], tb=True)
        g_wb_dn = _mm("branch_dn_dw", sv["o_dn"], dp_dn, ta=True, out_dtype=WIRE_DTYPE, tm=BRANCH_W)
        g_wb_pool = _mm("branch_pool_dw", sv["o_pool"], dp_pool, ta=True, out_dtype=WIRE_DTYPE, tm=BRANCH_W)
        g_wb_ret = _mm("branch_ret_dw", sv["o_ret"], dp_ret, ta=True, out_dtype=WIRE_DTYPE, tm=BRANCH_W)
        d_rq, d_rk, d_rv, d_rg, d_retw, _ = _scan_bwd("ret_bwd", _ret_group, sv["ret_rows"], 4, sv["ret_par"], sv["ret_states"], do_ret, s)
        d_pu, d_poolw, d_pools = _pool_bwd(sv["projm"], pool_w[l], pool_scale[l][None], do_pool, b_pu)
        d_qc, d_kc, d_vc, d_z, d_tail, d_alog, d_dtb, d_dnw = _scan_bwd("dn_bwd", _dn_group, sv["dn_rows"], 5, sv["dn_par"], sv["dn_states"], do_dn, s)
        d_qkv, d_convw = _conv_bwd(sv["projm"], sv["convw"], jnp.concatenate([d_qc, d_kc, d_vc], axis=1), b_qkv)
        dprojm = jnp.concatenate([dgates] + [t.astype(MXU_DTYPE) for t in (d_qkv, d_z, d_rq, d_rk, d_rv, d_rg, d_pu)], axis=1)
        dh_a = _mm("proj_main_dx", dprojm, wts["main"], tb=True, tk=col_tile)
        dh_b = _mm("proj_tail_dx", d_tail, wts["tail"], tb=True)
        g_main = _mm("proj_main_dw", sv["h"], dprojm, ta=True, out_dtype=WIRE_DTYPE, tn=col_tile)
        g_tail = _mm("proj_tail_dw", sv["h"], d_tail, ta=True, out_dtype=WIRE_DTYPE)
        gx, d_pre = norm_bwd("mix_pre_norm_bwd", sv["xin"], dh_a, dh_b, gmid, mix_pre_norm[l][None])
        canon = [(g_main[:, g0:g0 + src_tail], 0, src_tail), (g_tail[:, :2 * HEADS], src_tail, rest),
                 (g_main[:, g0 + src_tail:], rest, rest + 5 * BRANCH_W), (g_main[:, :g0], rest + 5 * BRANCH_W, d_in)]
        g_in = jnp.stack([jnp.concatenate(_column_slices(canon, k * d_in4, (k + 1) * d_in4), axis=1) for k in range(4)])
        split_cols = lambda t: jnp.stack([t[:, k * (d // 4):(k + 1) * (d // 4)] for k in range(4)])
        big_grads[l] = [g_in, split_cols(g_wb_dn), split_cols(g_wb_ret), split_cols(g_wb_pool), g_out.reshape(4, d // 4, d), g_gate, g_up, g_down]
        small_grads[l] = [d_pre[0], d_post[0], d_convw, d_alog[0, :HEADS], d_dtb[0, :HEADS], d_dnw[0], d_retw.reshape(-1), d_poolw, d_pools[0],
                          d_fpre[0], d_fpost[0]]

    reduced = [_reduce_scatter(big_grads[l]) for l in range(depth)]
    big_g = [jnp.stack([reduced[l][i] for l in range(depth)]) for i in range(8)]
    small_stacked = [jnp.stack([small_grads[l][i] for l in range(depth)]) for i in range(len(small_grads[0]))]
    (g_pre, g_post, g_conv_full, g_alog, g_dtb, g_dnw, g_retw, g_poolw, g_pools, g_fpre, g_fpost) = _all_reduce_arrays(small_stacked)
    g_conv = lax.dynamic_slice_in_dim(g_conv_full.reshape(depth, 4, 4, qkv_w // 4), xk, 1, axis=2)[:, :, 0, :]

    g_w_in, g_wbd, g_wbr, g_wbp, g_wo, g_fg, g_fu, g_fd = big_g
    large = [("w_in", w_in, g_w_in, m_w_in, v_w_in), ("w_branch_dn", w_branch_dn, g_wbd, m_w_branch_dn, v_w_branch_dn),
             ("w_branch_ret", w_branch_ret, g_wbr, m_w_branch_ret, v_w_branch_ret), ("w_branch_pool", w_branch_pool, g_wbp, m_w_branch_pool, v_w_branch_pool),
             ("w_out", w_out, g_wo, m_w_out, v_w_out), ("ffn_gate", ffn_gate, g_fg, m_ffn_gate, v_ffn_gate),
             ("ffn_up", ffn_up, g_fu, m_ffn_up, v_ffn_up), ("ffn_down", ffn_down, g_fd, m_ffn_down, v_ffn_down)]
    small = [("mix_pre_norm", mix_pre_norm, g_pre, m_mix_pre_norm, v_mix_pre_norm), ("mix_post_norm", mix_post_norm, g_post, m_mix_post_norm, v_mix_post_norm),
             ("dn_conv", dn_conv, g_conv, m_dn_conv, v_dn_conv), ("dn_A_log", dn_A_log, g_alog, m_dn_A_log, v_dn_A_log),
             ("dn_dt_bias", dn_dt_bias, g_dtb, m_dn_dt_bias, v_dn_dt_bias), ("dn_out_norm", dn_out_norm, g_dnw, m_dn_out_norm, v_dn_out_norm),
             ("ret_out_norm", ret_out_norm, g_retw, m_ret_out_norm, v_ret_out_norm), ("pool_w", pool_w, g_poolw, m_pool_w, v_pool_w),
             ("pool_scale", pool_scale, g_pools, m_pool_scale, v_pool_scale), ("ffn_pre_norm", ffn_pre_norm, g_fpre, m_ffn_pre_norm, v_ffn_pre_norm),
             ("ffn_post_norm", ffn_post_norm, g_fpost, m_ffn_post_norm, v_ffn_post_norm)]
    flat2 = lambda t: t.reshape(-1, t.shape[-1])
    sd, sm, sv_ = _adamw_small(*[[flat2(t[i]) for t in small] for i in (1, 2, 3, 4)])
    grads, deltas, new_m, new_v = {}, {}, {}, {}
    for name, w, g, m, v in large:
        grads[name] = g
        deltas[name], new_m[name], new_v[name] = _adamw_large("adamw_" + name, w, g, m, v)
    for i, (name, w, g, m, v) in enumerate(small):
        grads[name] = g.reshape(w.shape)
        deltas[name], new_m[name], new_v[name] = sd[i].reshape(w.shape), sm[i].reshape(w.shape), sv_[i].reshape(w.shape)
    order = ["mix_pre_norm", "mix_post_norm", "w_in", "dn_conv", "dn_A_log", "dn_dt_bias", "dn_out_norm", "ret_out_norm", "pool_w", "pool_scale",
             "w_branch_dn", "w_branch_ret", "w_branch_pool", "w_out", "ffn_pre_norm", "ffn_post_norm", "ffn_gate", "ffn_up", "ffn_down"]
    return (loss, gx[None], *[grads[n] for n in order], *[deltas[n] for n in order], *[new_m[n] for n in order], *[new_v[n] for n in order])
```

```python
import functools

import numpy as np
import jax
import jax.numpy as jnp
from jax import lax
from jax.experimental import pallas as pl
from jax.experimental.pallas import tpu as pltpu

F32 = jnp.float32
MXU_DTYPE = jnp.bfloat16
WIRE_DTYPE = jnp.bfloat16
HI = lax.Precision.HIGHEST
HI3 = lax.Precision.HIGH
SDS = jax.ShapeDtypeStruct
MESH = pl.DeviceIdType.MESH

CHUNK = 64
HEADS = 4
HEAD_DIM = 128
BRANCH_W = HEADS * HEAD_DIM
TAIL_W = 128
GROUP_ROWS = 512
RMS_EPS = 1e-6
GN_EPS = 1e-5
ROPE_BASE = 10000.0
POOL_WINDOWS = (2, 4, 8, 16)
VMEM_LIMIT_BYTES = 56 * 1024 * 1024

ADAM_LR = 0.001
ADAM_B1 = 0.9
ADAM_B2 = 0.999
ADAM_EPS = 1e-08
ADAM_WD = 0.01
ADAM_STEP = 10


def _params(sem=None):
    return pltpu.CompilerParams(dimension_semantics=sem, vmem_limit_bytes=VMEM_LIMIT_BYTES)


def _mx(v):
    return v.astype(MXU_DTYPE)


def _mm(name, a, b, *, ta=False, tb=False, contract_batch=False, out_dtype=F32, tm=1024, tn=None, tk=None):
    a3 = a if a.ndim == 3 else a[None]
    b3 = b if b.ndim == 3 else b[None]
    ba, bb = a3.shape[0], b3.shape[0]
    nb = max(ba, bb)
    k_dim, m_dim = (a3.shape[1], a3.shape[2]) if ta else (a3.shape[2], a3.shape[1])
    n_dim = b3.shape[1] if tb else b3.shape[2]
    assert k_dim == (b3.shape[2] if tb else b3.shape[1])
    tm = min(tm, m_dim)
    tn = n_dim if tn is None else min(tn, n_dim)
    tk = k_dim if tk is None else min(tk, k_dim)
    assert m_dim % tm == 0 and n_dim % tn == 0 and k_dim % tk == 0, (name, m_dim, n_dim, k_dim, tm, tn, tk)
    nbo, nbk = (1, nb) if contract_batch else (nb, 1)
    nk = k_dim // tk
    grid = (nbo, m_dim // tm, n_dim // tn, nbk, nk)

    def bsel(has, bo, kb):
        return (kb if contract_batch else bo) if has > 1 else 0

    a_spec = pl.BlockSpec((1, tk, tm) if ta else (1, tm, tk),
                          lambda bo, i, j, kb, kk: (bsel(ba, bo, kb),) + ((kk, i) if ta else (i, kk)))
    b_spec = pl.BlockSpec((1, tn, tk) if tb else (1, tk, tn),
                          lambda bo, i, j, kb, kk: (bsel(bb, bo, kb),) + ((j, kk) if tb else (kk, j)))
    o_spec = pl.BlockSpec((1, tm, tn), lambda bo, i, j, kb, kk: (bo, i, j))
    dims = (((0 if ta else 1,), (1 if tb else 0,)), ((), ()))
    steps = nbk * nk

    def body(a_ref, b_ref, o_ref, acc_ref):
        part = lax.dot_general(_mx(a_ref[0]), _mx(b_ref[0]), dims, preferred_element_type=F32)
        if steps == 1:
            o_ref[0] = part.astype(o_ref.dtype)
        else:
            step = pl.program_id(3) * nk + pl.program_id(4)

            @pl.when(step == 0)
            def _():
                acc_ref[...] = part

            @pl.when(step > 0)
            def _():
                acc_ref[...] += part

            @pl.when(step == steps - 1)
            def _():
                o_ref[0] = acc_ref[...].astype(o_ref.dtype)

    out = pl.pallas_call(
        body, name=name, grid=grid, in_specs=[a_spec, b_spec], out_specs=o_spec,
        out_shape=SDS((nbo, m_dim, n_dim), out_dtype),
        scratch_shapes=[pltpu.VMEM((tm, tn) if steps > 1 else (8, 128), F32)],
        compiler_params=_params(("parallel", "parallel", "parallel", "arbitrary", "arbitrary")),
    )(a3, b3)
    return out if nbo > 1 else out[0]


def _rowwise(name, fn, rows, fulls, out_rows, out_accs=(), *, tile):
    s = rows[0][0].shape[-2]
    tile = min(tile, s)
    assert s % tile == 0
    n_rows, n_full, n_or = len(rows), len(fulls), len(out_rows)

    def rspec(lead, width, cb):
        nl = len(lead)
        return pl.BlockSpec(tuple(lead) + (tile, width), lambda i, nl=nl, cb=cb: (0,) * nl + (i, cb))

    def fspec(shape):
        nd = len(shape)
        return pl.BlockSpec(tuple(shape), lambda i, nd=nd: (0,) * nd)

    in_specs = [rspec(arr.shape[:-2], w, cb) for arr, w, cb in rows] + [fspec(f.shape) for f in fulls]
    out_specs = [rspec(lead, w, 0) for lead, w, _ in out_rows] + [fspec(shape) for shape, _ in out_accs]
    out_shape = [SDS(tuple(lead) + (s, w), dt) for lead, w, dt in out_rows] + [SDS(tuple(shape), dt) for shape, dt in out_accs]

    def body(*refs):
        ins = [r[...] for r in refs[:n_rows + n_full]]
        outs = fn(*ins)
        o_refs = refs[n_rows + n_full:]
        for o_ref, val in zip(o_refs[:n_or], outs[:n_or]):
            o_ref[...] = val.astype(o_ref.dtype)
        first = pl.program_id(0) == 0
        for a_ref, val in zip(o_refs[n_or:], outs[n_or:]):
            @pl.when(first)
            def _(a_ref=a_ref, val=val):
                a_ref[...] = val.astype(a_ref.dtype)

            @pl.when(jnp.logical_not(first))
            def _(a_ref=a_ref, val=val):
                a_ref[...] += val.astype(a_ref.dtype)

    return pl.pallas_call(
        body, name=name, grid=(s // tile,), in_specs=in_specs, out_specs=out_specs, out_shape=out_shape,
        compiler_params=_params(("arbitrary",)),
    )(*[r[0] for r in rows], *fulls)


def _rms(x, w):
    return x * lax.rsqrt(jnp.mean(x * x, axis=-1, keepdims=True) + RMS_EPS) * w


def _silu(x):
    return x * jax.nn.sigmoid(x)


def _softplus(x):
    return jnp.maximum(x, 0.0) + jnp.log(1.0 + jnp.exp(-jnp.abs(x)))


def _pick_lane(row, idx):
    lane = lax.broadcasted_iota(jnp.int32, row.shape, row.ndim - 1)
    return jnp.sum(jnp.where(lane == idx, row, 0.0), axis=-1, keepdims=True)


def _pick_row(mat, idx):
    r = lax.broadcasted_iota(jnp.int32, mat.shape, 0)
    return jnp.sum(jnp.where(r == idx, mat, 0.0), axis=0, keepdims=True)


def _shift_rows(x, s):
    n = x.shape[0]
    rolled = pltpu.roll(x, s % n, 0)
    row = lax.broadcasted_iota(jnp.int32, x.shape, 0)
    keep = row >= s if s > 0 else row < n + s
    return jnp.where(keep, rolled, 0.0)


def _make_shift(s):
    @jax.custom_vjp
    def shift(x):
        return _shift_rows(x, s)

    shift.defvjp(lambda x: (_shift_rows(x, s), None), lambda _, g: (_shift_rows(g, -s),))
    return shift


_SHIFT = {s: _make_shift(s) for s in (1, 2, 3, 4, 8)}


@jax.custom_vjp
def _swap_halves(x):
    return pltpu.roll(x, HEAD_DIM // 2, 1)


_swap_halves.defvjp(lambda x: (pltpu.roll(x, HEAD_DIM // 2, 1), None), lambda _, g: (pltpu.roll(g, HEAD_DIM // 2, 1),))


def _bdot(a, b, ca, cb, precision=None):
    return lax.dot_general(a, b, (((ca,), (cb,)), ((0,), (0,))), precision=precision, preferred_element_type=F32)


def _dot(a, b, ca=1, cb=0):
    return lax.dot_general(a, b, (((ca,), (cb,)), ((), ())), preferred_element_type=F32)


def _conv_math(x, w0, w1, w2, w3):
    y = x * w3 + _SHIFT[1](x) * w2 + _SHIFT[2](x) * w1 + _SHIFT[3](x) * w0
    return _silu(y)


def _conv_fwd(projm, convw, base_block):
    s = projm.shape[0]
    nblk = convw.shape[1] // HEAD_DIM

    def body(x_ref, w_ref, o_ref):
        o_ref[...] = _conv_math(x_ref[...], *[w_ref[k:k + 1, :] for k in range(4)])

    return pl.pallas_call(
        body, name="dn_conv_fwd", grid=(nblk,),
        in_specs=[pl.BlockSpec((s, HEAD_DIM), lambda j: (0, base_block + j)), pl.BlockSpec((4, HEAD_DIM), lambda j: (0, j))],
        out_specs=pl.BlockSpec((s, HEAD_DIM), lambda j: (0, j)),
        out_shape=SDS((s, convw.shape[1]), F32), compiler_params=_params(("parallel",)),
    )(projm, convw)


def _conv_bwd(projm, convw, dy, base_block):
    s = projm.shape[0]
    nblk = convw.shape[1] // HEAD_DIM

    def body(x_ref, w_ref, dy_ref, dx_ref, dw_ref):
        _, vjp = jax.vjp(_conv_math, x_ref[...], *[w_ref[k:k + 1, :] for k in range(4)])
        grads = vjp(dy_ref[...])
        dx_ref[...] = grads[0]
        for k in range(4):
            dw_ref[k:k + 1, :] = jnp.sum(grads[1 + k], axis=0, keepdims=True)

    return pl.pallas_call(
        body, name="dn_conv_bwd", grid=(nblk,),
        in_specs=[pl.BlockSpec((s, HEAD_DIM), lambda j: (0, base_block + j)), pl.BlockSpec((4, HEAD_DIM), lambda j: (0, j)),
                  pl.BlockSpec((s, HEAD_DIM), lambda j: (0, j))],
        out_specs=[pl.BlockSpec((s, HEAD_DIM), lambda j: (0, j)), pl.BlockSpec((4, HEAD_DIM), lambda j: (0, j))],
        out_shape=[SDS((s, convw.shape[1]), F32), SDS(convw.shape, F32)], compiler_params=_params(("parallel",)),
    )(projm, convw, dy)


def _pool_math(grp, u, w, scale):
    w2 = u + _SHIFT[1](u)
    w4 = w2 + _SHIFT[2](w2)
    w8 = w4 + _SHIFT[4](w4)
    w16 = w8 + _SHIFT[8](w8)
    t1 = (lax.broadcasted_iota(jnp.int32, (u.shape[0], 1), 0) + 1).astype(F32)
    pooled = w16 / jnp.minimum(t1, float(POOL_WINDOWS[3]))
    for gi, acc in ((2, w8), (1, w4), (0, w2)):
        pooled = jnp.where(grp == gi, acc / jnp.minimum(t1, float(POOL_WINDOWS[gi])), pooled)
    mixed = pooled - u
    return _dot(_mx(mixed), _mx(w)) * scale


def _pool_fwd(projm, pool_w, pool_scale, base_block):
    s = projm.shape[0]

    def body(u_ref, w_ref, s_ref, o_ref):
        o_ref[...] = _pool_math(pl.program_id(0), u_ref[...], w_ref[0], s_ref[...]).astype(o_ref.dtype)

    return pl.pallas_call(
        body, name="pool_fwd", grid=(HEADS,),
        in_specs=[pl.BlockSpec((s, HEAD_DIM), lambda j: (0, base_block + j)), pl.BlockSpec((1, HEAD_DIM, HEAD_DIM), lambda j: (j, 0, 0)),
                  pl.BlockSpec((1, HEAD_DIM), lambda j: (0, j))],
        out_specs=pl.BlockSpec((s, HEAD_DIM), lambda j: (0, j)),
        out_shape=SDS((s, BRANCH_W), MXU_DTYPE), compiler_params=_params(("parallel",)),
    )(projm, pool_w, pool_scale)


def _pool_bwd(projm, pool_w, pool_scale, do, base_block):
    s = projm.shape[0]

    def body(u_ref, w_ref, s_ref, do_ref, du_ref, dw_ref, ds_ref):
        _, vjp = jax.vjp(functools.partial(_pool_math, pl.program_id(0)), u_ref[...], w_ref[0], s_ref[...])
        du, dw, ds = vjp(do_ref[...])
        du_ref[...] = du
        dw_ref[0] = dw
        ds_ref[...] = ds

    return pl.pallas_call(
        body, name="pool_bwd", grid=(HEADS,),
        in_specs=[pl.BlockSpec((s, HEAD_DIM), lambda j: (0, base_block + j)), pl.BlockSpec((1, HEAD_DIM, HEAD_DIM), lambda j: (j, 0, 0)),
                  pl.BlockSpec((1, HEAD_DIM), lambda j: (0, j)), pl.BlockSpec((s, HEAD_DIM), lambda j: (0, j))],
        out_specs=[pl.BlockSpec((s, HEAD_DIM), lambda j: (0, j)), pl.BlockSpec((1, HEAD_DIM, HEAD_DIM), lambda j: (j, 0, 0)),
                   pl.BlockSpec((1, HEAD_DIM), lambda j: (0, j))],
        out_shape=[SDS((s, BRANCH_W), F32), SDS(pool_w.shape, F32), SDS(pool_scale.shape, F32)],
        compiler_params=_params(("parallel",)),
    )(projm, pool_w, pool_scale, do)


def _scan_specs(s, rows, reverse):
    ng = s // GROUP_ROWS

    def gi(i):
        return ng - 1 - i if reverse else i

    specs = []
    for _, base, per_head in rows:
        specs.append(pl.BlockSpec((GROUP_ROWS, HEAD_DIM), lambda i, h, base=base, ph=per_head: (gi(i), base + (h if ph else 0))))
    return specs, gi, ng


def _scan_fwd(name, group_fn, rows, params, s):
    specs, gi, ng = _scan_specs(s, rows, False)
    n_in = len(rows) + len(params)
    pspecs = [pl.BlockSpec(p.shape, lambda i, h, nd=p.ndim: (0,) * nd) for p in params]

    def body(*refs):
        ins = [r[...] for r in refs[:n_in]]
        o_ref, saved_ref, st_ref = refs[n_in:]
        g, h = pl.program_id(0), pl.program_id(1)

        @pl.when(g == 0)
        def _():
            st_ref[h] = jnp.zeros((HEAD_DIM, HEAD_DIM), F32)

        state = st_ref[h]
        saved_ref[0, 0] = state
        out, new_state = group_fn(h, *ins, state)
        o_ref[...] = out.astype(o_ref.dtype)
        st_ref[h] = new_state

    return pl.pallas_call(
        body, name=name, grid=(ng, HEADS), in_specs=specs + pspecs,
        out_specs=[pl.BlockSpec((GROUP_ROWS, HEAD_DIM), lambda i, h: (i, h)),
                   pl.BlockSpec((1, 1, HEAD_DIM, HEAD_DIM), lambda i, h: (i, h, 0, 0))],
        out_shape=[SDS((s, BRANCH_W), MXU_DTYPE), SDS((ng, HEADS, HEAD_DIM, HEAD_DIM), F32)],
        scratch_shapes=[pltpu.VMEM((HEADS, HEAD_DIM, HEAD_DIM), F32)],
        compiler_params=_params(("arbitrary", "arbitrary")),
    )(*[r[0] for r in rows], *params)


def _scan_bwd(name, group_fn, rows, n_diff, params, saved, do, s):
    specs, gi, ng = _scan_specs(s, rows, True)
    n_rows, n_par = len(rows), len(params)
    pspecs = [pl.BlockSpec(p.shape, lambda i, h, nd=p.ndim: (0,) * nd) for p in params]
    in_specs = specs + pspecs + [pl.BlockSpec((1, 1, HEAD_DIM, HEAD_DIM), lambda i, h: (gi(i), h, 0, 0)),
                                 pl.BlockSpec((GROUP_ROWS, HEAD_DIM), lambda i, h: (gi(i), h))]
    out_specs, out_shape = [], []
    for _, _, per_head in rows[:n_diff]:
        out_specs.append(pl.BlockSpec((GROUP_ROWS, HEAD_DIM), lambda i, h, ph=per_head: (gi(i), h if ph else 0)))
        out_shape.append(SDS((s, BRANCH_W if per_head else HEAD_DIM), F32))
    for p in params:
        out_specs.append(pl.BlockSpec(p.shape, lambda i, h, nd=p.ndim: (0,) * nd))
        out_shape.append(SDS(p.shape, F32))

    def body(*refs):
        row_vals = [r[...] for r in refs[:n_rows]]
        par_vals = [r[...] for r in refs[n_rows:n_rows + n_par]]
        saved_ref, do_ref = refs[n_rows + n_par:n_rows + n_par + 2]
        outs = refs[n_rows + n_par + 2:-1]
        dst_ref = refs[-1]
        i, h = pl.program_id(0), pl.program_id(1)

        @pl.when(i == 0)
        def _():
            dst_ref[h] = jnp.zeros((HEAD_DIM, HEAD_DIM), F32)

        consts = row_vals[n_diff:]

        def f(*args):
            return group_fn(h, *args[:n_diff], *consts, *args[n_diff:])

        _, vjp = jax.vjp(f, *row_vals[:n_diff], *par_vals, saved_ref[0, 0])
        grads = vjp((do_ref[...], dst_ref[h]))
        dst_ref[h] = grads[-1]
        for k, (_, _, per_head) in enumerate(rows[:n_diff]):
            if per_head:
                outs[k][...] = grads[k]
            else:
                @pl.when(h == 0)
                def _(k=k):
                    outs[k][...] = grads[k]

                @pl.when(h > 0)
                def _(k=k):
                    outs[k][...] += grads[k]
        first = jnp.logical_and(i == 0, h == 0)
        for k in range(n_par):
            @pl.when(first)
            def _(k=k):
                outs[n_diff + k][...] = grads[n_diff + k]

            @pl.when(jnp.logical_not(first))
            def _(k=k):
                outs[n_diff + k][...] += grads[n_diff + k]

    return pl.pallas_call(
        body, name=name, grid=(ng, HEADS), in_specs=in_specs, out_specs=out_specs, out_shape=out_shape,
        scratch_shapes=[pltpu.VMEM((HEADS, HEAD_DIM, HEAD_DIM), F32)],
        compiler_params=_params(("arbitrary", "arbitrary")),
    )(*[r[0] for r in rows], *params, saved, do)


def _dn_group(h, qc, kc, vc, z, tail, alog, dtb, dnw, state):
    c = CHUNK
    nb = qc.shape[0] // c
    beta = jax.nn.sigmoid(_pick_lane(tail, h))
    g = -jnp.exp(_pick_lane(alog, h)) * _softplus(_pick_lane(tail, HEADS + h) + _pick_lane(dtb, h))
    q = qc * lax.rsqrt(jnp.sum(qc * qc, axis=-1, keepdims=True) + 1e-6) * (HEAD_DIM ** -0.5)
    k = kc * lax.rsqrt(jnp.sum(kc * kc, axis=-1, keepdims=True) + 1e-6)
    q, k, v = (t.reshape(nb, c, HEAD_DIM) for t in (q, k, vc))
    beta = beta.reshape(nb, c, 1)
    g = g.reshape(nb, c, 1)
    g_wide = jnp.broadcast_to(g, (nb, c, HEAD_DIM))
    g_sq = jnp.broadcast_to(g, (nb, c, c))
    ri = lax.broadcasted_iota(jnp.int32, (nb, c, c), 1)
    ci = lax.broadcasted_iota(jnp.int32, (nb, c, c), 2)
    causal = ri >= ci
    lower = causal.astype(F32)
    upper = (ri <= ci).astype(F32)
    cum_wide = _bdot(lower, g_wide, 2, 1, HI)
    cum_i = _bdot(lower, g_sq, 2, 1, HI)
    cum_j = _bdot(g_sq, upper, 1, 1, HI)
    decay = jnp.where(causal, jnp.exp(jnp.where(causal, cum_i - cum_j, 0.0)), 0.0)
    k_beta = k * beta
    m = jnp.where(ri > ci, _bdot(k_beta, k, 2, 2, HI3) * decay, 0.0)
    p = -m
    inv = jnp.where(ri == ci, 1.0, 0.0) + p
    for _ in range(5):
        p = _bdot(p, p, 2, 1, HI3)
        inv = inv + _bdot(inv, p, 2, 1, HI3)
    e_cum = jnp.exp(cum_wide)
    u = _bdot(inv, v * beta, 2, 1, HI3)
    w = _bdot(inv, k_beta * e_cum, 2, 1, HI3)
    attn = jnp.where(causal, _bdot(q, k, 2, 2, HI3) * decay, 0.0)
    q_dec = q * e_cum
    g_last = jnp.sum(g_wide, axis=1, keepdims=True)
    k_tail = k * jnp.exp(g_last - cum_wide)
    d_last = jnp.exp(g_last)
    outs = []
    for n in range(nb):
        st = _mx(state)
        v_new = u[n] - _dot(_mx(w[n]), st)
        outs.append(_dot(_mx(q_dec[n]), st) + _dot(_mx(attn[n]), _mx(v_new)))
        state = state * d_last[n] + _dot(_mx(k_tail[n]), _mx(v_new), 0, 0)
    o = jnp.concatenate(outs, axis=0)
    o = o * lax.rsqrt(jnp.mean(o * o, axis=-1, keepdims=True) + RMS_EPS) * dnw * _silu(z)
    return o, state


def _ret_group(h, q, k, v, gate, cosf, sinf, retw, lgam, state):
    c = CHUNK
    nb = q.shape[0] // c
    lg = _pick_lane(lgam, h)
    rq = q * cosf + _swap_halves(q) * sinf
    rk = (k * cosf + _swap_halves(k) * sinf) * (HEAD_DIM ** -0.5)
    rq, rk, v3 = (t.reshape(nb, c, HEAD_DIM) for t in (rq, rk, v))
    ri = lax.broadcasted_iota(jnp.int32, (nb, c, c), 1)
    ci = lax.broadcasted_iota(jnp.int32, (nb, c, c), 2)
    dmask = jnp.exp(jnp.abs(ri - ci).astype(F32) * lg)
    scores = _bdot(_mx(rq), _mx(rk), 2, 2) * dmask
    o_inner = _bdot(_mx(scores), _mx(v3), 2, 1)
    pos = lax.broadcasted_iota(jnp.int32, (nb, c, 1), 1).astype(F32)
    xi = jnp.exp((pos + 1.0) * lg)
    zeta = jnp.exp((c - 1.0 - pos) * lg)
    kv = _bdot(_mx(rk * zeta), _mx(v3), 1, 1)
    chunk_decay = jnp.exp(float(c) * lg)
    entering = []
    for n in range(nb):
        entering.append(state)
        state = state * chunk_decay + kv[n]
    r_prev = jnp.stack(entering, axis=0)
    o = (o_inner + _bdot(_mx(rq * xi), _mx(r_prev), 2, 1)).reshape(nb * c, HEAD_DIM)
    mu = jnp.mean(o, axis=-1, keepdims=True)
    var = jnp.mean(jnp.square(o - mu), axis=-1, keepdims=True)
    o = (o - mu) * lax.rsqrt(var + GN_EPS) * _pick_row(retw, h) * _silu(gate)
    return o, state


def _place():
    x, y, c = lax.axis_index("x"), lax.axis_index("y"), lax.axis_index("c")
    chips = [(x, 1 - y), (1 - x, y), (1 - x, 1 - y)]
    return x, y, c, chips


_HBM = pl.BlockSpec(memory_space=pltpu.HBM)


LOCAL_COPY_CHUNKS = 4


def _half(c, rows, align):
    hr = rows // 2
    return pl.ds(pl.multiple_of(c * hr, align), hr), pl.ds(pl.multiple_of((1 - c) * hr, align), hr)


def _gather_weights(shards):
    n = len(shards)

    def body(*refs):
        x_refs, o_refs = refs[:n], refs[n:2 * n]
        send_sems, recv_sems, local_sems = refs[2 * n:]
        x, y, c, chips = _place()
        sib = (x, y, 1 - c)
        me = 2 * x + y
        slots = [2 * chip[0] + chip[1] for chip in chips]

        def copy(i, sem, src, slot, rows, to):
            return pltpu.make_async_remote_copy(src_ref=src, dst_ref=o_refs[i].at[slot, rows], send_sem=send_sems.at[6 * i + sem],
                                                recv_sem=recv_sems.at[6 * i + sem], device_id=to, device_id_type=MESH)

        halves = [_half(c, x_refs[i].shape[0], 16) for i in range(n)]
        own = []
        for i in range(n):
            step = x_refs[i].shape[0] // LOCAL_COPY_CHUNKS
            for q in range(LOCAL_COPY_CHUNKS):
                rows = pl.ds(q * step, step)
                own.append(pltpu.make_async_copy(x_refs[i].at[rows], o_refs[i].at[me, rows], local_sems.at[LOCAL_COPY_CHUNKS * i + q]))
        first = [copy(i, j, x_refs[i].at[halves[i][0]], me, halves[i][0], (*chip, c)) for i in range(n) for j, chip in enumerate(chips)]
        for cp in first + own:
            cp.start()
        passed = []
        for j, chip in enumerate(chips):
            for i in range(n):
                copy(i, j, x_refs[i].at[halves[i][0]], slots[j], halves[i][0], (*chip, c)).wait_recv()
                fwd = copy(i, 3 + j, o_refs[i].at[slots[j], halves[i][0]], slots[j], halves[i][0], sib)
                fwd.start()
                passed.append(fwd)
        for j in range(3):
            for i in range(n):
                copy(i, 3 + j, x_refs[i].at[halves[i][0]], slots[j], halves[i][1], sib).wait_recv()
        for cp in first + passed:
            cp.wait_send()
        for cp in own:
            cp.wait()

    return pl.pallas_call(
        body, name="gather_weights", out_shape=[SDS((4,) + t.shape, t.dtype) for t in shards], in_specs=[_HBM] * n, out_specs=[_HBM] * n,
        scratch_shapes=[pltpu.SemaphoreType.DMA((6 * n,)), pltpu.SemaphoreType.DMA((6 * n,)), pltpu.SemaphoreType.DMA((LOCAL_COPY_CHUNKS * n,))],
    )(*shards)


def _swap_with_sibling(grads):
    n = len(grads)

    def body(*refs):
        g_refs, o_refs = refs[:n], refs[n:2 * n]
        send_sems, recv_sems = refs[2 * n:]
        x, y, c, _ = _place()
        copies = [pltpu.make_async_remote_copy(src_ref=g_refs[i].at[:, _half(c, g_refs[i].shape[1], 16)[1]], dst_ref=o_refs[i],
                                               send_sem=send_sems.at[i], recv_sem=recv_sems.at[i], device_id=(x, y, 1 - c), device_id_type=MESH)
                  for i in range(n)]
        for cp in copies:
            cp.start()
        for cp in copies:
            cp.wait()

    return pl.pallas_call(
        body, name="grad_to_sibling", out_shape=[SDS((4, t.shape[1] // 2, t.shape[2]), t.dtype) for t in grads],
        in_specs=[_HBM] * n, out_specs=[_HBM] * n, scratch_shapes=[pltpu.SemaphoreType.DMA((n,)), pltpu.SemaphoreType.DMA((n,))],
    )(*grads)


def _scatter_to_chips(parts):
    n = len(parts)

    def body(*refs):
        p_refs, o_refs = refs[:n], refs[n:2 * n]
        send_sems, recv_sems = refs[2 * n:]
        x, y, c, chips = _place()
        copies = [pltpu.make_async_remote_copy(src_ref=p_refs[i].at[2 * chip[0] + chip[1]], dst_ref=o_refs[i].at[j], send_sem=send_sems.at[3 * i + j],
                                               recv_sem=recv_sems.at[3 * i + j], device_id=(*chip, c), device_id_type=MESH)
                  for j, chip in enumerate(chips) for i in range(n)]
        for cp in copies:
            cp.start()
        for cp in copies:
            cp.wait()

    return pl.pallas_call(
        body, name="grad_to_chips", out_shape=[SDS((3,) + t.shape[1:], t.dtype) for t in parts], in_specs=[_HBM] * n, out_specs=[_HBM] * n,
        scratch_shapes=[pltpu.SemaphoreType.DMA((3 * n,)), pltpu.SemaphoreType.DMA((3 * n,))],
    )(*parts)


def _join_halves(bufs):
    n = len(bufs)

    def body(*refs):
        o_refs = refs[n:2 * n]
        send_sems, recv_sems = refs[2 * n:]
        x, y, c, _ = _place()
        sends, recvs = [], []
        for i in range(n):
            mine, theirs = _half(c, o_refs[i].shape[0], 8)
            sends.append(pltpu.make_async_remote_copy(src_ref=o_refs[i].at[mine], dst_ref=o_refs[i].at[mine], send_sem=send_sems.at[i],
                                                      recv_sem=recv_sems.at[i], device_id=(x, y, 1 - c), device_id_type=MESH))
            recvs.append(pltpu.make_async_remote_copy(src_ref=o_refs[i].at[mine], dst_ref=o_refs[i].at[theirs], send_sem=send_sems.at[i],
                                                      recv_sem=recv_sems.at[i], device_id=(x, y, 1 - c), device_id_type=MESH))
        for cp in sends:
            cp.start()
        for cp in recvs:
            cp.wait_recv()
        for cp in sends:
            cp.wait_send()

    return pl.pallas_call(
        body, name="grad_join_halves", out_shape=[SDS(t.shape, t.dtype) for t in bufs], in_specs=[_HBM] * n, out_specs=[_HBM] * n,
        input_output_aliases={i: i for i in range(n)},
        scratch_shapes=[pltpu.SemaphoreType.DMA((n,)), pltpu.SemaphoreType.DMA((n,))],
    )(*bufs)


def _add_sibling(name, grads, from_sib, place, n_tiles):
    n = len(grads)

    def body(place_ref, *refs):
        for i in range(n):
            refs[2 * n + i][...] = (refs[i][...].astype(F32) + refs[n + i][...].astype(F32)).astype(refs[2 * n + i].dtype)

    blocks = [(4, t.shape[1] // n_tiles, t.shape[2]) for t in from_sib]
    return pl.pallas_call(
        body, name=name,
        grid_spec=pltpu.PrefetchScalarGridSpec(
            num_scalar_prefetch=1, grid=(n_tiles,),
            in_specs=[pl.BlockSpec(b, lambda t, p: (0, p[0] * n_tiles + t, 0)) for b in blocks] + [pl.BlockSpec(b, lambda t, p: (0, t, 0)) for b in blocks],
            out_specs=[pl.BlockSpec(b, lambda t, p: (0, t, 0)) for b in blocks]),
        out_shape=[SDS(t.shape, t.dtype) for t in from_sib], compiler_params=_params(("parallel",)),
    )(place, *grads, *from_sib)


def _add_chips(name, parts, from_chips, place, n_tiles):
    n = len(parts)

    def body(place_ref, *refs):
        for i in range(n):
            p_ref, r_ref = refs[i], refs[n + i]
            refs[2 * n + i][...] = ((p_ref[0].astype(F32) + r_ref[0].astype(F32)) + r_ref[1].astype(F32)) + r_ref[2].astype(F32)

    tiles = [(t.shape[1] // n_tiles, t.shape[2]) for t in parts]
    return pl.pallas_call(
        body, name=name,
        grid_spec=pltpu.PrefetchScalarGridSpec(
            num_scalar_prefetch=1, grid=(n_tiles,),
            in_specs=[pl.BlockSpec((1,) + b, lambda t, p: (p[1], t, 0)) for b in tiles] + [pl.BlockSpec((3,) + b, lambda t, p: (0, t, 0)) for b in tiles],
            out_specs=[pl.BlockSpec(b, lambda t, p: (p[0] * n_tiles + t, 0)) for b in tiles]),
        out_shape=[SDS((2 * t.shape[1], t.shape[2]), F32) for t in parts], compiler_params=_params(("parallel",)),
    )(place, *parts, *from_chips)


def _reduce_scatter(grads):
    place = jnp.stack([lax.axis_index("c"), 2 * lax.axis_index("x") + lax.axis_index("y")]).astype(jnp.int32)
    from_sib = _swap_with_sibling(grads)
    part = _add_sibling("grad_add_sibling_wide", grads[:1], from_sib[:1], place, 4) + _add_sibling("grad_add_sibling", grads[1:], from_sib[1:], place, 2)
    from_chips = _scatter_to_chips(part)
    halves = _add_chips("grad_add_chips_wide", part[:1], from_chips[:1], place, 4) + _add_chips("grad_add_chips", part[1:], from_chips[1:], place, 2)
    return _join_halves(halves)


def _all_reduce_small(x):
    _, n, wd = x.shape

    def body(x_ref, o_ref, buf, send1, recv1, send2, recv2):
        mx, my, mc = lax.axis_index("x"), lax.axis_index("y"), lax.axis_index("c")
        me = 4 * mx + 2 * my + mc
        peers = []
        for rel in range(1, 8):
            px, py, pc = (1 - mx if rel & 4 else mx), (1 - my if rel & 2 else my), (1 - mc if rel & 1 else mc)
            peers.append(((px, py, pc), 4 * px + 2 * py + pc))
        out = [pltpu.make_async_remote_copy(src_ref=x_ref.at[pidx], dst_ref=buf.at[j], send_sem=send1.at[j], recv_sem=recv1.at[j],
                                            device_id=peer, device_id_type=MESH) for j, (peer, pidx) in enumerate(peers)]
        for cp in out:
            cp.start()
        for cp in out:
            cp.wait()
        acc = x_ref[me]
        for j in range(7):
            acc = acc + buf[j]
        o_ref[me] = acc
        back = [pltpu.make_async_remote_copy(src_ref=o_ref.at[me], dst_ref=o_ref.at[me], send_sem=send2.at[j], recv_sem=recv2.at[j],
                                             device_id=peer, device_id_type=MESH) for j, (peer, _) in enumerate(peers)]
        for cp in back:
            cp.start()
        for j, (peer, pidx) in enumerate(peers):
            pltpu.make_async_remote_copy(src_ref=o_ref.at[me], dst_ref=o_ref.at[pidx], send_sem=send2.at[j], recv_sem=recv2.at[j],
                                         device_id=peer, device_id_type=MESH).wait_recv()
        for cp in back:
            cp.wait_send()

    vm = pl.BlockSpec(memory_space=pltpu.VMEM)
    return pl.pallas_call(
        body, name="all_reduce_small", out_shape=SDS(x.shape, F32), in_specs=[vm], out_specs=vm,
        scratch_shapes=[pltpu.VMEM((7, n, wd), F32)] + [pltpu.SemaphoreType.DMA((7,))] * 4,
        compiler_params=pltpu.CompilerParams(vmem_limit_bytes=VMEM_LIMIT_BYTES),
    )(x)


def _rows128(t):
    if t.size % 128 == 0:
        return t.reshape(-1, 128)
    assert t.size < 128
    return jnp.pad(t.reshape(1, -1), ((0, 0), (0, 128 - t.size)))


def _all_reduce_arrays(arrays):
    rows = [_rows128(t) for t in arrays]
    total = sum(r.shape[0] for r in rows)
    padded = -(-total // 64) * 64
    buf = jnp.concatenate(rows + [jnp.zeros((padded - total, 128), F32)], axis=0).reshape(8, padded // 8, 128)
    summed = _all_reduce_small(buf).reshape(padded, 128)
    out, off = [], 0
    for t, r in zip(arrays, rows):
        blk = summed[off:off + r.shape[0]]
        out.append(blk.reshape(t.shape) if t.size % 128 == 0 else blk[0, :t.size].reshape(t.shape))
        off += r.shape[0]
    return out


def _column_slices(sources, lo, hi):
    out = []
    for arr, a, b in sources:
        s, e = max(lo, a), min(hi, b)
        if s < e:
            out.append(arr[:, s - a:e - a])
    return out


def _adamw_math(w, g, m, v):
    m = ADAM_B1 * m + (1.0 - ADAM_B1) * g
    v = ADAM_B2 * v + (1.0 - ADAM_B2) * jnp.square(g)
    m_hat = m / (1.0 - ADAM_B1 ** ADAM_STEP)
    v_hat = v / (1.0 - ADAM_B2 ** ADAM_STEP)
    delta = -ADAM_LR * (m_hat / (jnp.sqrt(v_hat) + ADAM_EPS) + ADAM_WD * w)
    return delta, m, v


def _adamw_large(name, w, g, m, v):
    shape = w.shape
    cols = shape[-1]
    rows = int(np.prod(shape[:-1]))
    tile = 256 if rows % 256 == 0 else rows
    args = [t.reshape(rows, cols) for t in (w, g, m, v)]
    outs = _rowwise(name, _adamw_math, [(a, cols, 0) for a in args], [], [((), cols, F32)] * 3, tile=tile)
    return [o.reshape(shape) for o in outs]


def _adamw_small(ws, gs, ms, vs):
    n = len(ws)

    def body(*refs):
        for i in range(n):
            d, m, v = _adamw_math(refs[i][...], refs[n + i][...], refs[2 * n + i][...], refs[3 * n + i][...])
            refs[4 * n + i][...] = d
            refs[5 * n + i][...] = m
            refs[6 * n + i][...] = v

    outs = pl.pallas_call(body, name="adamw_small", out_shape=[SDS(w.shape, F32) for w in ws] * 3,
                          compiler_params=pltpu.CompilerParams(vmem_limit_bytes=VMEM_LIMIT_BYTES))(*ws, *gs, *ms, *vs)
    return outs[:n], outs[n:2 * n], outs[2 * n:]


def _rotary_tables(s):
    half = HEAD_DIM // 2
    inv = ROPE_BASE ** (-jnp.arange(half, dtype=F32) / half)
    ang = jnp.arange(s, dtype=F32)[:, None] * inv[None, :]
    cos, sin = jnp.cos(ang), jnp.sin(ang)
    return jnp.concatenate([cos, cos], axis=-1), jnp.concatenate([-sin, sin], axis=-1)


def kernel(x, mix_pre_norm, mix_post_norm, w_in, dn_conv, dn_A_log, dn_dt_bias, dn_out_norm, ret_out_norm, pool_w, pool_scale, w_branch_dn, w_branch_ret, w_branch_pool, w_out, ffn_pre_norm, ffn_post_norm, ffn_gate, ffn_up, ffn_down, loss_target, m_mix_pre_norm, m_mix_post_norm, m_w_in, m_dn_conv, m_dn_A_log, m_dn_dt_bias, m_dn_out_norm, m_ret_out_norm, m_pool_w, m_pool_scale, m_w_branch_dn, m_w_branch_ret, m_w_branch_pool, m_w_out, m_ffn_pre_norm, m_ffn_post_norm, m_ffn_gate, m_ffn_up, m_ffn_down, v_mix_pre_norm, v_mix_post_norm, v_w_in, v_dn_conv, v_dn_A_log, v_dn_dt_bias, v_dn_out_norm, v_ret_out_norm, v_pool_w, v_pool_scale, v_w_branch_dn, v_w_branch_ret, v_w_branch_pool, v_w_out, v_ffn_pre_norm, v_ffn_post_norm, v_ffn_gate, v_ffn_up, v_ffn_down):
    depth = w_in.shape[0]
    s, d = x.shape[1], x.shape[2]
    d_in4 = w_in.shape[2]
    d_in = 4 * d_in4
    f4 = ffn_gate.shape[2]
    qkv_w = 3 * BRANCH_W
    g0 = 3 * d
    b_qkv, b_z = g0 // HEAD_DIM, (g0 + qkv_w) // HEAD_DIM
    b_rq, b_rk, b_rv, b_rg, b_pu = ((g0 + qkv_w + BRANCH_W * i) // HEAD_DIM for i in range(1, 6))
    main_w = g0 + qkv_w + 6 * BRANCH_W
    src_tail = qkv_w + BRANCH_W
    assert d_in == main_w + 2 * HEADS
    col_tile = 1536 if main_w % 1536 == 0 else 128
    tile = 256
    xk = lax.axis_index("x") * 2 + lax.axis_index("y")

    cosf, sinf = _rotary_tables(s)
    lgam = jnp.zeros((1, HEAD_DIM), F32).at[0, :HEADS].set(jnp.log1p(-jnp.exp2(-5.0 - jnp.arange(HEADS, dtype=F32))))

    def pad_lanes(vec):
        return jnp.zeros((1, HEAD_DIM), F32).at[0, :vec.shape[0]].set(vec)

    gathered = [_gather_weights([t[l].astype(WIRE_DTYPE) for t in (w_in, w_branch_dn, w_branch_ret, w_branch_pool, w_out, ffn_gate, ffn_up, ffn_down)])
                for l in range(depth)]
    conv_place = lax.dynamic_update_slice(jnp.zeros((4,) + dn_conv.shape, F32), dn_conv[None], (xk, 0, 0, 0))
    conv_sum = _all_reduce_arrays([conv_place * (1 - lax.axis_index("c")).astype(F32)])[0]
    conv_all = jnp.concatenate([conv_sum[k] for k in range(4)], axis=-1)
    rest = src_tail + 2 * HEADS

    def layer_weights(l):
        wi, wbd, wbr, wbp, wo, wg, wu, wdn = gathered[l]
        pieces = [(wi[k], k * d_in4, (k + 1) * d_in4) for k in range(4)]
        w_main = jnp.concatenate(_column_slices(pieces, rest + 5 * BRANCH_W, d_in) + _column_slices(pieces, 0, src_tail)
                                 + _column_slices(pieces, rest, rest + 5 * BRANCH_W), axis=1)
        w_tail = jnp.concatenate(_column_slices(pieces, src_tail, rest) + [jnp.zeros((d, TAIL_W - 2 * HEADS), wi.dtype)], axis=1)
        wb = [jnp.concatenate([t[k] for k in range(4)], axis=1) for t in (wbd, wbp, wbr)]
        return dict(main=w_main, tail=w_tail, wb=wb, out=wo.reshape(d, d), gate=wg, up=wu, down=wdn)

    def norm_fwd(name, xin, w):
        return _rowwise(name, lambda xv, wv: (_rms(xv, wv),), [(xin, d, 0)], [w], [((), d, MXU_DTYPE)], tile=tile)[0]

    def resnorm_fwd(name, xin, t, w):
        return _rowwise(name, lambda xv, tv, wv: (xv + _rms(tv, wv),), [(xin, d, 0), (t, d, 0)], [w], [((), d, F32)], tile=tile)[0]

    def merge_math(gates, p_dn, p_pool, p_ret):
        gt = jax.nn.sigmoid(gates)
        return gt[:, :d] * p_dn + gt[:, d:2 * d] * p_pool + gt[:, 2 * d:] * p_ret

    def swiglu_math(u, v):
        return _silu(u) * v

    xs, saved = [x[0]], []
    for l in range(depth):
        wts = layer_weights(l)
        xin = xs[-1]
        h = norm_fwd("mix_pre_norm_fwd", xin, mix_pre_norm[l][None])
        projm = _mm("proj_main", h, wts["main"], tn=col_tile)
        projt = _mm("proj_tail", h, wts["tail"])
        convw = conv_all[l]
        qkv_c = _conv_fwd(projm, convw, b_qkv)
        dn_par = [pad_lanes(dn_A_log[l]), pad_lanes(dn_dt_bias[l]), dn_out_norm[l][None]]
        dn_rows = [(qkv_c, 0, True), (qkv_c, HEADS, True), (qkv_c, 2 * HEADS, True), (projm, b_z, True), (projt, 0, False)]
        o_dn, dn_states = _scan_fwd("dn_fwd", _dn_group, dn_rows, dn_par, s)
        ret_par = [ret_out_norm[l].reshape(HEADS, HEAD_DIM), lgam]
        ret_rows = [(projm, b_rq, True), (projm, b_rk, True), (projm, b_rv, True), (projm, b_rg, True), (cosf, 0, False), (sinf, 0, False)]
        o_ret, ret_states = _scan_fwd("ret_fwd", _ret_group, ret_rows, ret_par, s)
        o_pool = _pool_fwd(projm, pool_w[l], pool_scale[l][None], b_pu)
        p_dn = _mm("branch_dn", o_dn, wts["wb"][0])
        p_pool = _mm("branch_pool", o_pool, wts["wb"][1])
        p_ret = _mm("branch_ret", o_ret, wts["wb"][2])
        ymix = _rowwise("merge_fwd", lambda *a: (merge_math(*a),), [(projm, 3 * d, 0), (p_dn, d, 0), (p_pool, d, 0), (p_ret, d, 0)], [],
                        [((), d, MXU_DTYPE)], tile=tile)[0]
        mo = _mm("mix_out", ymix, wts["out"])
        xmid = resnorm_fwd("mix_post_norm_fwd", xin, mo, mix_post_norm[l][None])
        h2 = norm_fwd("ffn_pre_norm_fwd", xmid, ffn_pre_norm[l][None])
        fu = _mm("ffn_gate", h2, wts["gate"])
        fv = _mm("ffn_up", h2, wts["up"])
        act = _rowwise("swiglu_fwd", lambda u, v: (swiglu_math(u, v),), [(fu, f4, 0), (fv, f4, 0)], [], [((4,), f4, MXU_DTYPE)], tile=tile)[0]
        fo = _mm("ffn_down", act, wts["down"], contract_batch=True)
        xout = resnorm_fwd("ffn_post_norm_fwd", xmid, fo, ffn_post_norm[l][None])
        xs.append(xout)
        saved.append(dict(wts=wts, h=h, projm=projm, convw=convw, dn_rows=dn_rows, dn_par=dn_par, dn_states=dn_states,
                          ret_rows=ret_rows, ret_par=ret_par, ret_states=ret_states, o_dn=o_dn, o_pool=o_pool, o_ret=o_ret,
                          p=(p_dn, p_pool, p_ret), ymix=ymix, mo=mo, xmid=xmid, h2=h2, fu=fu, fv=fv, act=act, fo=fo, xin=xin))

    def loss_math(y, t):
        diff = y - t
        part = 0.5 * jnp.sum(jnp.sum(diff * diff, axis=-1, keepdims=True) / d, axis=0, keepdims=True)
        return diff / d, jnp.broadcast_to(part, (8, 128))

    gx, loss_part = _rowwise("loss_head", loss_math, [(xs[-1], d, 0), (loss_target[0], d, 0)], [], [((), d, F32)], [((8, 128), F32)], tile=tile)
    loss = lax.psum(loss_part[0, 0], ("x", "y", "c"))

    def resnorm_bwd(name, t, gout, w):
        def fn(tv, gv, wv):
            _, vjp = jax.vjp(_rms, tv, wv)
            return vjp(gv)
        return _rowwise(name, fn, [(t, d, 0), (gout, d, 0)], [w], [((), d, F32)], [((1, d), F32)], tile=tile)

    def norm_bwd(name, xin, dh_a, dh_b, gout, w):
        def fn(xv, da, db, gv, wv):
            _, vjp = jax.vjp(_rms, xv, wv)
            dx_, dw_ = vjp(da + db)
            return gv + dx_, dw_
        return _rowwise(name, fn, [(xin, d, 0), (dh_a, d, 0), (dh_b, d, 0), (gout, d, 0)], [w], [((), d, F32)], [((1, d), F32)], tile=tile)

    def swiglu_bwd(u, v, da):
        _, vjp = jax.vjp(swiglu_math, u, v)
        return vjp(da)

    def merge_bwd(gates, p_dn, p_pool, p_ret, dyv):
        _, vjp = jax.vjp(merge_math, gates, p_dn, p_pool, p_ret)
        return vjp(dyv)

    big_grads, small_grads = [None] * depth, [None] * depth
    for l in reversed(range(depth)):
        sv = saved[l]
        wts = sv["wts"]
        dfo, d_fpost = resnorm_bwd("ffn_post_norm_bwd", sv["fo"], gx, ffn_post_norm[l][None])
        dact = _mm("ffn_down_dx", dfo, wts["down"], tb=True)
        g_down = _mm("ffn_down_dw", sv["act"], dfo, ta=True, out_dtype=WIRE_DTYPE, tm=f4)
        du, dv = _rowwise("swiglu_bwd", swiglu_bwd, [(sv["fu"], f4, 0), (sv["fv"], f4, 0), (dact, f4, 0)], [],
                          [((4,), f4, MXU_DTYPE), ((4,), f4, MXU_DTYPE)], tile=tile)
        g_gate = _mm("ffn_gate_dw", sv["h2"], du, ta=True, out_dtype=WIRE_DTYPE)
        g_up = _mm("ffn_up_dw", sv["h2"], dv, ta=True, out_dtype=WIRE_DTYPE)
        dh2_a = _mm("ffn_gate_dx", du, wts["gate"], tb=True, contract_batch=True)
        dh2_b = _mm("ffn_up_dx", dv, wts["up"], tb=True, contract_batch=True)
        gmid, d_fpre = norm_bwd("ffn_pre_norm_bwd", sv["xmid"], dh2_a, dh2_b, gx, ffn_pre_norm[l][None])
        dmo, d_post = resnorm_bwd("mix_post_norm_bwd", sv["mo"], gmid, mix_post_norm[l][None])
        dy = _mm("mix_out_dx", dmo, wts["out"], tb=True)
        g_out = _mm("mix_out_dw", sv["ymix"], dmo, ta=True, out_dtype=WIRE_DTYPE)
        dgates, dp_dn, dp_pool, dp_ret = _rowwise(
            "merge_bwd", merge_bwd, [(sv["projm"], 3 * d, 0), (sv["p"][0], d, 0), (sv["p"][1], d, 0), (sv["p"][2], d, 0), (dy, d, 0)], [],
            [((), 3 * d, MXU_DTYPE), ((), d, MXU_DTYPE), ((), d, MXU_DTYPE), ((), d, MXU_DTYPE)], tile=tile)
        do_dn = _mm("branch_dn_dx", dp_dn, wts["wb"][0], tb=True)
        do_pool = _mm("branch_pool_dx", dp_pool, wts["wb"][1], tb=True)
        do_ret = _mm("branch_ret_dx", dp_ret, wts["wb"][2], tb=True)
        g_wb_dn = _mm("branch_dn_dw", sv["o_dn"], dp_dn, ta=True, out_dtype=WIRE_DTYPE, tm=BRANCH_W)
        g_wb_pool = _mm("branch_pool_dw", sv["o_pool"], dp_pool, ta=True, out_dtype=WIRE_DTYPE, tm=BRANCH_W)
        g_wb_ret = _mm("branch_ret_dw", sv["o_ret"], dp_ret, ta=True, out_dtype=WIRE_DTYPE, tm=BRANCH_W)
        d_rq, d_rk, d_rv, d_rg, d_retw, _ = _scan_bwd("ret_bwd", _ret_group, sv["ret_rows"], 4, sv["ret_par"], sv["ret_states"], do_ret, s)
        d_pu, d_poolw, d_pools = _pool_bwd(sv["projm"], pool_w[l], pool_scale[l][None], do_pool, b_pu)
        d_qc, d_kc, d_vc, d_z, d_tail, d_alog, d_dtb, d_dnw = _scan_bwd("dn_bwd", _dn_group, sv["dn_rows"], 5, sv["dn_par"], sv["dn_states"], do_dn, s)
        d_qkv, d_convw = _conv_bwd(sv["projm"], sv["convw"], jnp.concatenate([d_qc, d_kc, d_vc], axis=1), b_qkv)
        dprojm = jnp.concatenate([dgates] + [t.astype(MXU_DTYPE) for t in (d_qkv, d_z, d_rq, d_rk, d_rv, d_rg, d_pu)], axis=1)
        dh_a = _mm("proj_main_dx", dprojm, wts["main"], tb=True, tk=col_tile)
        dh_b = _mm("proj_tail_dx", d_tail, wts["tail"], tb=True)
        g_main = _mm("proj_main_dw", sv["h"], dprojm, ta=True, out_dtype=WIRE_DTYPE, tn=col_tile)
        g_tail = _mm("proj_tail_dw", sv["h"], d_tail, ta=True, out_dtype=WIRE_DTYPE)
        gx, d_pre = norm_bwd("mix_pre_norm_bwd", sv["xin"], dh_a, dh_b, gmid, mix_pre_norm[l][None])
        canon = [(g_main[:, g0:g0 + src_tail], 0, src_tail), (g_tail[:, :2 * HEADS], src_tail, rest),
                 (g_main[:, g0 + src_tail:], rest, rest + 5 * BRANCH_W), (g_main[:, :g0], rest + 5 * BRANCH_W, d_in)]
        g_in = jnp.stack([jnp.concatenate(_column_slices(canon, k * d_in4, (k + 1) * d_in4), axis=1) for k in range(4)])
        split_cols = lambda t: jnp.stack([t[:, k * (d // 4):(k + 1) * (d // 4)] for k in range(4)])
        big_grads[l] = [g_in, split_cols(g_wb_dn), split_cols(g_wb_ret), split_cols(g_wb_pool), g_out.reshape(4, d // 4, d), g_gate, g_up, g_down]
        small_grads[l] = [d_pre[0], d_post[0], d_convw, d_alog[0, :HEADS], d_dtb[0, :HEADS], d_dnw[0], d_retw.reshape(-1), d_poolw, d_pools[0],
                          d_fpre[0], d_fpost[0]]

    reduced = [_reduce_scatter(big_grads[l]) for l in range(depth)]
    big_g = [jnp.stack([reduced[l][i] for l in range(depth)]) for i in range(8)]
    small_stacked = [jnp.stack([small_grads[l][i] for l in range(depth)]) for i in range(len(small_grads[0]))]
    (g_pre, g_post, g_conv_full, g_alog, g_dtb, g_dnw, g_retw, g_poolw, g_pools, g_fpre, g_fpost) = _all_reduce_arrays(small_stacked)
    g_conv = lax.dynamic_slice_in_dim(g_conv_full.reshape(depth, 4, 4, qkv_w // 4), xk, 1, axis=2)[:, :, 0, :]

    g_w_in, g_wbd, g_wbr, g_wbp, g_wo, g_fg, g_fu, g_fd = big_g
    large = [("w_in", w_in, g_w_in, m_w_in, v_w_in), ("w_branch_dn", w_branch_dn, g_wbd, m_w_branch_dn, v_w_branch_dn),
             ("w_branch_ret", w_branch_ret, g_wbr, m_w_branch_ret, v_w_branch_ret), ("w_branch_pool", w_branch_pool, g_wbp, m_w_branch_pool, v_w_branch_pool),
             ("w_out", w_out, g_wo, m_w_out, v_w_out), ("ffn_gate", ffn_gate, g_fg, m_ffn_gate, v_ffn_gate),
             ("ffn_up", ffn_up, g_fu, m_ffn_up, v_ffn_up), ("ffn_down", ffn_down, g_fd, m_ffn_down, v_ffn_down)]
    small = [("mix_pre_norm", mix_pre_norm, g_pre, m_mix_pre_norm, v_mix_pre_norm), ("mix_post_norm", mix_post_norm, g_post, m_mix_post_norm, v_mix_post_norm),
             ("dn_conv", dn_conv, g_conv, m_dn_conv, v_dn_conv), ("dn_A_log", dn_A_log, g_alog, m_dn_A_log, v_dn_A_log),
             ("dn_dt_bias", dn_dt_bias, g_dtb, m_dn_dt_bias, v_dn_dt_bias), ("dn_out_norm", dn_out_norm, g_dnw, m_dn_out_norm, v_dn_out_norm),
             ("ret_out_norm", ret_out_norm, g_retw, m_ret_out_norm, v_ret_out_norm), ("pool_w", pool_w, g_poolw, m_pool_w, v_pool_w),
             ("pool_scale", pool_scale, g_pools, m_pool_scale, v_pool_scale), ("ffn_pre_norm", ffn_pre_norm, g_fpre, m_ffn_pre_norm, v_ffn_pre_norm),
             ("ffn_post_norm", ffn_post_norm, g_fpost, m_ffn_post_norm, v_ffn_post_norm)]
    flat2 = lambda t: t.reshape(-1, t.shape[-1])
    sd, sm, sv_ = _adamw_small(*[[flat2(t[i]) for t in small] for i in (1, 2, 3, 4)])
    grads, deltas, new_m, new_v = {}, {}, {}, {}
    for name, w, g, m, v in large:
        grads[name] = g
        deltas[name], new_m[name], new_v[name] = _adamw_large("adamw_" + name, w, g, m, v)
    for i, (name, w, g, m, v) in enumerate(small):
        grads[name] = g.reshape(w.shape)
        deltas[name], new_m[name], new_v[name] = sd[i].reshape(w.shape), sm[i].reshape(w.shape), sv_[i].reshape(w.shape)
    order = ["mix_pre_norm", "mix_post_norm", "w_in", "dn_conv", "dn_A_log", "dn_dt_bias", "dn_out_norm", "ret_out_norm", "pool_w", "pool_scale",
             "w_branch_dn", "w_branch_ret", "w_branch_pool", "w_out", "ffn_pre_norm", "ffn_post_norm", "ffn_gate", "ffn_up", "ffn_down"]
    return (loss, gx[None], *[grads[n] for n in order], *[deltas[n] for n in order], *[new_m[n] for n in order], *[new_v[n] for n in order])
```

```python
import functools

import numpy as np
import jax
import jax.numpy as jnp
from jax import lax
from jax.experimental import pallas as pl
from jax.experimental.pallas import tpu as pltpu

F32 = jnp.float32
MXU_DTYPE = jnp.bfloat16
WIRE_DTYPE = jnp.bfloat16
HI = lax.Precision.HIGHEST
HI3 = lax.Precision.HIGH
SDS = jax.ShapeDtypeStruct
MESH = pl.DeviceIdType.MESH

CHUNK = 64
HEADS = 4
HEAD_DIM = 128
BRANCH_W = HEADS * HEAD_DIM
TAIL_W = 128
GROUP_ROWS = 1024
RMS_EPS = 1e-6
GN_EPS = 1e-5
ROPE_BASE = 10000.0
POOL_WINDOWS = (2, 4, 8, 16)
VMEM_LIMIT_BYTES = 56 * 1024 * 1024

ADAM_LR = 0.001
ADAM_B1 = 0.9
ADAM_B2 = 0.999
ADAM_EPS = 1e-08
ADAM_WD = 0.01
ADAM_STEP = 10


def _params(sem=None):
    return pltpu.CompilerParams(dimension_semantics=sem, vmem_limit_bytes=VMEM_LIMIT_BYTES)


def _mx(v):
    return v.astype(MXU_DTYPE)


def _mm(name, a, b, *, ta=False, tb=False, contract_batch=False, out_dtype=F32, tm=1024, tn=None, tk=None):
    a3 = a if a.ndim == 3 else a[None]
    b3 = b if b.ndim == 3 else b[None]
    ba, bb = a3.shape[0], b3.shape[0]
    nb = max(ba, bb)
    k_dim, m_dim = (a3.shape[1], a3.shape[2]) if ta else (a3.shape[2], a3.shape[1])
    n_dim = b3.shape[1] if tb else b3.shape[2]
    assert k_dim == (b3.shape[2] if tb else b3.shape[1])
    tm = min(tm, m_dim)
    tn = n_dim if tn is None else min(tn, n_dim)
    tk = k_dim if tk is None else min(tk, k_dim)
    assert m_dim % tm == 0 and n_dim % tn == 0 and k_dim % tk == 0, (name, m_dim, n_dim, k_dim, tm, tn, tk)
    nbo, nbk = (1, nb) if contract_batch else (nb, 1)
    nk = k_dim // tk
    grid = (nbo, m_dim // tm, n_dim // tn, nbk, nk)

    def bsel(has, bo, kb):
        return (kb if contract_batch else bo) if has > 1 else 0

    a_spec = pl.BlockSpec((1, tk, tm) if ta else (1, tm, tk),
                          lambda bo, i, j, kb, kk: (bsel(ba, bo, kb),) + ((kk, i) if ta else (i, kk)))
    b_spec = pl.BlockSpec((1, tn, tk) if tb else (1, tk, tn),
                          lambda bo, i, j, kb, kk: (bsel(bb, bo, kb),) + ((j, kk) if tb else (kk, j)))
    o_spec = pl.BlockSpec((1, tm, tn), lambda bo, i, j, kb, kk: (bo, i, j))
    dims = (((0 if ta else 1,), (1 if tb else 0,)), ((), ()))
    steps = nbk * nk

    def body(a_ref, b_ref, o_ref, acc_ref):
        part = lax.dot_general(_mx(a_ref[0]), _mx(b_ref[0]), dims, preferred_element_type=F32)
        if steps == 1:
            o_ref[0] = part.astype(o_ref.dtype)
        else:
            step = pl.program_id(3) * nk + pl.program_id(4)

            @pl.when(step == 0)
            def _():
                acc_ref[...] = part

            @pl.when(step > 0)
            def _():
                acc_ref[...] += part

            @pl.when(step == steps - 1)
            def _():
                o_ref[0] = acc_ref[...].astype(o_ref.dtype)

    out = pl.pallas_call(
        body, name=name, grid=grid, in_specs=[a_spec, b_spec], out_specs=o_spec,
        out_shape=SDS((nbo, m_dim, n_dim), out_dtype),
        scratch_shapes=[pltpu.VMEM((tm, tn) if steps > 1 else (8, 128), F32)],
        compiler_params=_params(("parallel", "parallel", "parallel", "arbitrary", "arbitrary")),
    )(a3, b3)
    return out if nbo > 1 else out[0]


def _rowwise(name, fn, rows, fulls, out_rows, out_accs=(), *, tile):
    s = rows[0][0].shape[-2]
    tile = min(tile, s)
    assert s % tile == 0
    n_rows, n_full, n_or = len(rows), len(fulls), len(out_rows)

    def rspec(lead, width, cb):
        nl = len(lead)
        return pl.BlockSpec(tuple(lead) + (tile, width), lambda i, nl=nl, cb=cb: (0,) * nl + (i, cb))

    def fspec(shape):
        nd = len(shape)
        return pl.BlockSpec(tuple(shape), lambda i, nd=nd: (0,) * nd)

    in_specs = [rspec(arr.shape[:-2], w, cb) for arr, w, cb in rows] + [fspec(f.shape) for f in fulls]
    out_specs = [rspec(lead, w, 0) for lead, w, _ in out_rows] + [fspec(shape) for shape, _ in out_accs]
    out_shape = [SDS(tuple(lead) + (s, w), dt) for lead, w, dt in out_rows] + [SDS(tuple(shape), dt) for shape, dt in out_accs]

    def body(*refs):
        ins = [r[...] for r in refs[:n_rows + n_full]]
        outs = fn(*ins)
        o_refs = refs[n_rows + n_full:]
        for o_ref, val in zip(o_refs[:n_or], outs[:n_or]):
            o_ref[...] = val.astype(o_ref.dtype)
        first = pl.program_id(0) == 0
        for a_ref, val in zip(o_refs[n_or:], outs[n_or:]):
            @pl.when(first)
            def _(a_ref=a_ref, val=val):
                a_ref[...] = val.astype(a_ref.dtype)

            @pl.when(jnp.logical_not(first))
            def _(a_ref=a_ref, val=val):
                a_ref[...] += val.astype(a_ref.dtype)

    return pl.pallas_call(
        body, name=name, grid=(s // tile,), in_specs=in_specs, out_specs=out_specs, out_shape=out_shape,
        compiler_params=_params(("arbitrary",)),
    )(*[r[0] for r in rows], *fulls)


def _rms(x, w):
    return x * lax.rsqrt(jnp.mean(x * x, axis=-1, keepdims=True) + RMS_EPS) * w


def _silu(x):
    return x * jax.nn.sigmoid(x)


def _softplus(x):
    return jnp.maximum(x, 0.0) + jnp.log(1.0 + jnp.exp(-jnp.abs(x)))


def _pick_lane(row, idx):
    lane = lax.broadcasted_iota(jnp.int32, row.shape, row.ndim - 1)
    return jnp.sum(jnp.where(lane == idx, row, 0.0), axis=-1, keepdims=True)


def _pick_row(mat, idx):
    r = lax.broadcasted_iota(jnp.int32, mat.shape, 0)
    return jnp.sum(jnp.where(r == idx, mat, 0.0), axis=0, keepdims=True)


def _shift_rows(x, s):
    n = x.shape[0]
    rolled = pltpu.roll(x, s % n, 0)
    row = lax.broadcasted_iota(jnp.int32, x.shape, 0)
    keep = row >= s if s > 0 else row < n + s
    return jnp.where(keep, rolled, 0.0)


def _make_shift(s):
    @jax.custom_vjp
    def shift(x):
        return _shift_rows(x, s)

    shift.defvjp(lambda x: (_shift_rows(x, s), None), lambda _, g: (_shift_rows(g, -s),))
    return shift


_SHIFT = {s: _make_shift(s) for s in (1, 2, 3, 4, 8)}


@jax.custom_vjp
def _swap_halves(x):
    return pltpu.roll(x, HEAD_DIM // 2, 1)


_swap_halves.defvjp(lambda x: (pltpu.roll(x, HEAD_DIM // 2, 1), None), lambda _, g: (pltpu.roll(g, HEAD_DIM // 2, 1),))


def _bdot(a, b, ca, cb, precision=None):
    return lax.dot_general(a, b, (((ca,), (cb,)), ((0,), (0,))), precision=precision, preferred_element_type=F32)


def _dot(a, b, ca=1, cb=0):
    return lax.dot_general(a, b, (((ca,), (cb,)), ((), ())), preferred_element_type=F32)


def _conv_math(x, w0, w1, w2, w3):
    y = x * w3 + _SHIFT[1](x) * w2 + _SHIFT[2](x) * w1 + _SHIFT[3](x) * w0
    return _silu(y)


def _conv_fwd(projm, convw, base_block):
    s = projm.shape[0]
    nblk = convw.shape[1] // HEAD_DIM

    def body(x_ref, w_ref, o_ref):
        o_ref[...] = _conv_math(x_ref[...], *[w_ref[k:k + 1, :] for k in range(4)])

    return pl.pallas_call(
        body, name="dn_conv_fwd", grid=(nblk,),
        in_specs=[pl.BlockSpec((s, HEAD_DIM), lambda j: (0, base_block + j)), pl.BlockSpec((4, HEAD_DIM), lambda j: (0, j))],
        out_specs=pl.BlockSpec((s, HEAD_DIM), lambda j: (0, j)),
        out_shape=SDS((s, convw.shape[1]), F32), compiler_params=_params(("parallel",)),
    )(projm, convw)


def _conv_bwd(projm, convw, dy, base_block):
    s = projm.shape[0]
    nblk = convw.shape[1] // HEAD_DIM

    def body(x_ref, w_ref, dy_ref, dx_ref, dw_ref):
        _, vjp = jax.vjp(_conv_math, x_ref[...], *[w_ref[k:k + 1, :] for k in range(4)])
        grads = vjp(dy_ref[...])
        dx_ref[...] = grads[0]
        for k in range(4):
            dw_ref[k:k + 1, :] = jnp.sum(grads[1 + k], axis=0, keepdims=True)

    return pl.pallas_call(
        body, name="dn_conv_bwd", grid=(nblk,),
        in_specs=[pl.BlockSpec((s, HEAD_DIM), lambda j: (0, base_block + j)), pl.BlockSpec((4, HEAD_DIM), lambda j: (0, j)),
                  pl.BlockSpec((s, HEAD_DIM), lambda j: (0, j))],
        out_specs=[pl.BlockSpec((s, HEAD_DIM), lambda j: (0, j)), pl.BlockSpec((4, HEAD_DIM), lambda j: (0, j))],
        out_shape=[SDS((s, convw.shape[1]), F32), SDS(convw.shape, F32)], compiler_params=_params(("parallel",)),
    )(projm, convw, dy)


def _pool_math(grp, u, w, scale):
    w2 = u + _SHIFT[1](u)
    w4 = w2 + _SHIFT[2](w2)
    w8 = w4 + _SHIFT[4](w4)
    w16 = w8 + _SHIFT[8](w8)
    t1 = (lax.broadcasted_iota(jnp.int32, (u.shape[0], 1), 0) + 1).astype(F32)
    pooled = w16 / jnp.minimum(t1, float(POOL_WINDOWS[3]))
    for gi, acc in ((2, w8), (1, w4), (0, w2)):
        pooled = jnp.where(grp == gi, acc / jnp.minimum(t1, float(POOL_WINDOWS[gi])), pooled)
    mixed = pooled - u
    return _dot(_mx(mixed), _mx(w)) * scale


def _pool_fwd(projm, pool_w, pool_scale, base_block):
    s = projm.shape[0]

    def body(u_ref, w_ref, s_ref, o_ref):
        o_ref[...] = _pool_math(pl.program_id(0), u_ref[...], w_ref[0], s_ref[...]).astype(o_ref.dtype)

    return pl.pallas_call(
        body, name="pool_fwd", grid=(HEADS,),
        in_specs=[pl.BlockSpec((s, HEAD_DIM), lambda j: (0, base_block + j)), pl.BlockSpec((1, HEAD_DIM, HEAD_DIM), lambda j: (j, 0, 0)),
                  pl.BlockSpec((1, HEAD_DIM), lambda j: (0, j))],
        out_specs=pl.BlockSpec((s, HEAD_DIM), lambda j: (0, j)),
        out_shape=SDS((s, BRANCH_W), MXU_DTYPE), compiler_params=_params(("parallel",)),
    )(projm, pool_w, pool_scale)


def _pool_bwd(projm, pool_w, pool_scale, do, base_block):
    s = projm.shape[0]

    def body(u_ref, w_ref, s_ref, do_ref, du_ref, dw_ref, ds_ref):
        _, vjp = jax.vjp(functools.partial(_pool_math, pl.program_id(0)), u_ref[...], w_ref[0], s_ref[...])
        du, dw, ds = vjp(do_ref[...])
        du_ref[...] = du
        dw_ref[0] = dw
        ds_ref[...] = ds

    return pl.pallas_call(
        body, name="pool_bwd", grid=(HEADS,),
        in_specs=[pl.BlockSpec((s, HEAD_DIM), lambda j: (0, base_block + j)), pl.BlockSpec((1, HEAD_DIM, HEAD_DIM), lambda j: (j, 0, 0)),
                  pl.BlockSpec((1, HEAD_DIM), lambda j: (0, j)), pl.BlockSpec((s, HEAD_DIM), lambda j: (0, j))],
        out_specs=[pl.BlockSpec((s, HEAD_DIM), lambda j: (0, j)), pl.BlockSpec((1, HEAD_DIM, HEAD_DIM), lambda j: (j, 0, 0)),
                   pl.BlockSpec((1, HEAD_DIM), lambda j: (0, j))],
        out_shape=[SDS((s, BRANCH_W), F32), SDS(pool_w.shape, F32), SDS(pool_scale.shape, F32)],
        compiler_params=_params(("parallel",)),
    )(projm, pool_w, pool_scale, do)


def _scan_specs(s, rows, reverse):
    ng = s // GROUP_ROWS

    def gi(i):
        return ng - 1 - i if reverse else i

    specs = []
    for _, base, per_head in rows:
        specs.append(pl.BlockSpec((GROUP_ROWS, HEAD_DIM), lambda i, h, base=base, ph=per_head: (gi(i), base + (h if ph else 0))))
    return specs, gi, ng


def _scan_fwd(name, group_fn, rows, params, s):
    specs, gi, ng = _scan_specs(s, rows, False)
    n_in = len(rows) + len(params)
    pspecs = [pl.BlockSpec(p.shape, lambda i, h, nd=p.ndim: (0,) * nd) for p in params]

    def body(*refs):
        ins = [r[...] for r in refs[:n_in]]
        o_ref, saved_ref, st_ref = refs[n_in:]
        g, h = pl.program_id(0), pl.program_id(1)

        @pl.when(g == 0)
        def _():
            st_ref[h] = jnp.zeros((HEAD_DIM, HEAD_DIM), F32)

        state = st_ref[h]
        saved_ref[0, 0] = state
        out, new_state = group_fn(h, *ins, state)
        o_ref[...] = out.astype(o_ref.dtype)
        st_ref[h] = new_state

    return pl.pallas_call(
        body, name=name, grid=(ng, HEADS), in_specs=specs + pspecs,
        out_specs=[pl.BlockSpec((GROUP_ROWS, HEAD_DIM), lambda i, h: (i, h)),
                   pl.BlockSpec((1, 1, HEAD_DIM, HEAD_DIM), lambda i, h: (i, h, 0, 0))],
        out_shape=[SDS((s, BRANCH_W), MXU_DTYPE), SDS((ng, HEADS, HEAD_DIM, HEAD_DIM), F32)],
        scratch_shapes=[pltpu.VMEM((HEADS, HEAD_DIM, HEAD_DIM), F32)],
        compiler_params=_params(("arbitrary", "arbitrary")),
    )(*[r[0] for r in rows], *params)


def _scan_bwd(name, group_fn, rows, n_diff, params, saved, do, s):
    specs, gi, ng = _scan_specs(s, rows, True)
    n_rows, n_par = len(rows), len(params)
    pspecs = [pl.BlockSpec(p.shape, lambda i, h, nd=p.ndim: (0,) * nd) for p in params]
    in_specs = specs + pspecs + [pl.BlockSpec((1, 1, HEAD_DIM, HEAD_DIM), lambda i, h: (gi(i), h, 0, 0)),
                                 pl.BlockSpec((GROUP_ROWS, HEAD_DIM), lambda i, h: (gi(i), h))]
    out_specs, out_shape = [], []
    for _, _, per_head in rows[:n_diff]:
        out_specs.append(pl.BlockSpec((GROUP_ROWS, HEAD_DIM), lambda i, h, ph=per_head: (gi(i), h if ph else 0)))
        out_shape.append(SDS((s, BRANCH_W if per_head else HEAD_DIM), F32))
    for p in params:
        out_specs.append(pl.BlockSpec(p.shape, lambda i, h, nd=p.ndim: (0,) * nd))
        out_shape.append(SDS(p.shape, F32))

    def body(*refs):
        row_vals = [r[...] for r in refs[:n_rows]]
        par_vals = [r[...] for r in refs[n_rows:n_rows + n_par]]
        saved_ref, do_ref = refs[n_rows + n_par:n_rows + n_par + 2]
        outs = refs[n_rows + n_par + 2:-1]
        dst_ref = refs[-1]
        i, h = pl.program_id(0), pl.program_id(1)

        @pl.when(i == 0)
        def _():
            dst_ref[h] = jnp.zeros((HEAD_DIM, HEAD_DIM), F32)

        consts = row_vals[n_diff:]

        def f(*args):
            return group_fn(h, *args[:n_diff], *consts, *args[n_diff:])

        _, vjp = jax.vjp(f, *row_vals[:n_diff], *par_vals, saved_ref[0, 0])
        grads = vjp((do_ref[...], dst_ref[h]))
        dst_ref[h] = grads[-1]
        for k, (_, _, per_head) in enumerate(rows[:n_diff]):
            if per_head:
                outs[k][...] = grads[k]
            else:
                @pl.when(h == 0)
                def _(k=k):
                    outs[k][...] = grads[k]

                @pl.when(h > 0)
                def _(k=k):
                    outs[k][...] += grads[k]
        first = jnp.logical_and(i == 0, h == 0)
        for k in range(n_par):
            @pl.when(first)
            def _(k=k):
                outs[n_diff + k][...] = grads[n_diff + k]

            @pl.when(jnp.logical_not(first))
            def _(k=k):
                outs[n_diff + k][...] += grads[n_diff + k]

    return pl.pallas_call(
        body, name=name, grid=(ng, HEADS), in_specs=in_specs, out_specs=out_specs, out_shape=out_shape,
        scratch_shapes=[pltpu.VMEM((HEADS, HEAD_DIM, HEAD_DIM), F32)],
        compiler_params=_params(("arbitrary", "arbitrary")),
    )(*[r[0] for r in rows], *params, saved, do)


def _dn_group(h, qc, kc, vc, z, tail, alog, dtb, dnw, state):
    c = CHUNK
    nb = qc.shape[0] // c
    beta = jax.nn.sigmoid(_pick_lane(tail, h))
    g = -jnp.exp(_pick_lane(alog, h)) * _softplus(_pick_lane(tail, HEADS + h) + _pick_lane(dtb, h))
    q = qc * lax.rsqrt(jnp.sum(qc * qc, axis=-1, keepdims=True) + 1e-6) * (HEAD_DIM ** -0.5)
    k = kc * lax.rsqrt(jnp.sum(kc * kc, axis=-1, keepdims=True) + 1e-6)
    q, k, v = (t.reshape(nb, c, HEAD_DIM) for t in (q, k, vc))
    beta = beta.reshape(nb, c, 1)
    g = g.reshape(nb, c, 1)
    g_wide = jnp.broadcast_to(g, (nb, c, HEAD_DIM))
    g_sq = jnp.broadcast_to(g, (nb, c, c))
    ri = lax.broadcasted_iota(jnp.int32, (nb, c, c), 1)
    ci = lax.broadcasted_iota(jnp.int32, (nb, c, c), 2)
    causal = ri >= ci
    lower = causal.astype(F32)
    upper = (ri <= ci).astype(F32)
    cum_wide = _bdot(lower, g_wide, 2, 1, HI)
    cum_i = _bdot(lower, g_sq, 2, 1, HI)
    cum_j = _bdot(g_sq, upper, 1, 1, HI)
    decay = jnp.where(causal, jnp.exp(jnp.where(causal, cum_i - cum_j, 0.0)), 0.0)
    k_beta = k * beta
    m = jnp.where(ri > ci, _bdot(k_beta, k, 2, 2, HI3) * decay, 0.0)
    p = -m
    inv = jnp.where(ri == ci, 1.0, 0.0) + p
    for _ in range(5):
        p = _bdot(p, p, 2, 1, HI3)
        inv = inv + _bdot(inv, p, 2, 1, HI3)
    e_cum = jnp.exp(cum_wide)
    u = _bdot(inv, v * beta, 2, 1, HI3)
    w = _bdot(inv, k_beta * e_cum, 2, 1, HI3)
    attn = jnp.where(causal, _bdot(q, k, 2, 2, HI3) * decay, 0.0)
    q_dec = q * e_cum
    g_last = jnp.sum(g_wide, axis=1, keepdims=True)
    k_tail = k * jnp.exp(g_last - cum_wide)
    d_last = jnp.exp(g_last)
    outs = []
    for n in range(nb):
        st = _mx(state)
        v_new = u[n] - _dot(_mx(w[n]), st)
        outs.append(_dot(_mx(q_dec[n]), st) + _dot(_mx(attn[n]), _mx(v_new)))
        state = state * d_last[n] + _dot(_mx(k_tail[n]), _mx(v_new), 0, 0)
    o = jnp.concatenate(outs, axis=0)
    o = o * lax.rsqrt(jnp.mean(o * o, axis=-1, keepdims=True) + RMS_EPS) * dnw * _silu(z)
    return o, state


def _ret_group(h, q, k, v, gate, cosf, sinf, retw, lgam, state):
    c = CHUNK
    nb = q.shape[0] // c
    lg = _pick_lane(lgam, h)
    rq = q * cosf + _swap_halves(q) * sinf
    rk = (k * cosf + _swap_halves(k) * sinf) * (HEAD_DIM ** -0.5)
    rq, rk, v3 = (t.reshape(nb, c, HEAD_DIM) for t in (rq, rk, v))
    ri = lax.broadcasted_iota(jnp.int32, (nb, c, c), 1)
    ci = lax.broadcasted_iota(jnp.int32, (nb, c, c), 2)
    dmask = jnp.exp(jnp.abs(ri - ci).astype(F32) * lg)
    scores = _bdot(_mx(rq), _mx(rk), 2, 2) * dmask
    o_inner = _bdot(_mx(scores), _mx(v3), 2, 1)
    pos = lax.broadcasted_iota(jnp.int32, (nb, c, 1), 1).astype(F32)
    xi = jnp.exp((pos + 1.0) * lg)
    zeta = jnp.exp((c - 1.0 - pos) * lg)
    kv = _bdot(_mx(rk * zeta), _mx(v3), 1, 1)
    chunk_decay = jnp.exp(float(c) * lg)
    entering = []
    for n in range(nb):
        entering.append(state)
        state = state * chunk_decay + kv[n]
    r_prev = jnp.stack(entering, axis=0)
    o = (o_inner + _bdot(_mx(rq * xi), _mx(r_prev), 2, 1)).reshape(nb * c, HEAD_DIM)
    mu = jnp.mean(o, axis=-1, keepdims=True)
    var = jnp.mean(jnp.square(o - mu), axis=-1, keepdims=True)
    o = (o - mu) * lax.rsqrt(var + GN_EPS) * _pick_row(retw, h) * _silu(gate)
    return o, state


def _place():
    x, y, c = lax.axis_index("x"), lax.axis_index("y"), lax.axis_index("c")
    chips = [(x, 1 - y), (1 - x, y), (1 - x, 1 - y)]
    return x, y, c, chips


_HBM = pl.BlockSpec(memory_space=pltpu.HBM)


LOCAL_COPY_CHUNKS = 4


def _half(c, rows, align):
    hr = rows // 2
    return pl.ds(pl.multiple_of(c * hr, align), hr), pl.ds(pl.multiple_of((1 - c) * hr, align), hr)


def _gather_weights(shards):
    n = len(shards)

    def body(*refs):
        x_refs, o_refs = refs[:n], refs[n:2 * n]
        send_sems, recv_sems, local_sems = refs[2 * n:]
        x, y, c, chips = _place()
        sib = (x, y, 1 - c)
        me = 2 * x + y
        slots = [2 * chip[0] + chip[1] for chip in chips]

        def copy(i, sem, src, slot, rows, to):
            return pltpu.make_async_remote_copy(src_ref=src, dst_ref=o_refs[i].at[slot, rows], send_sem=send_sems.at[6 * i + sem],
                                                recv_sem=recv_sems.at[6 * i + sem], device_id=to, device_id_type=MESH)

        halves = [_half(c, x_refs[i].shape[0], 16) for i in range(n)]
        own = []
        for i in range(n):
            step = x_refs[i].shape[0] // LOCAL_COPY_CHUNKS
            for q in range(LOCAL_COPY_CHUNKS):
                rows = pl.ds(q * step, step)
                own.append(pltpu.make_async_copy(x_refs[i].at[rows], o_refs[i].at[me, rows], local_sems.at[LOCAL_COPY_CHUNKS * i + q]))
        first = [copy(i, j, x_refs[i].at[halves[i][0]], me, halves[i][0], (*chip, c)) for i in range(n) for j, chip in enumerate(chips)]
        for cp in first + own:
            cp.start()
        passed = []
        for j, chip in enumerate(chips):
            for i in range(n):
                copy(i, j, x_refs[i].at[halves[i][0]], slots[j], halves[i][0], (*chip, c)).wait_recv()
                fwd = copy(i, 3 + j, o_refs[i].at[slots[j], halves[i][0]], slots[j], halves[i][0], sib)
                fwd.start()
                passed.append(fwd)
        for j in range(3):
            for i in range(n):
                copy(i, 3 + j, x_refs[i].at[halves[i][0]], slots[j], halves[i][1], sib).wait_recv()
        for cp in first + passed:
            cp.wait_send()
        for cp in own:
            cp.wait()

    return pl.pallas_call(
        body, name="gather_weights", out_shape=[SDS((4,) + t.shape, t.dtype) for t in shards], in_specs=[_HBM] * n, out_specs=[_HBM] * n,
        scratch_shapes=[pltpu.SemaphoreType.DMA((6 * n,)), pltpu.SemaphoreType.DMA((6 * n,)), pltpu.SemaphoreType.DMA((LOCAL_COPY_CHUNKS * n,))],
    )(*shards)


def _swap_with_sibling(grads):
    n = len(grads)

    def body(*refs):
        g_refs, o_refs = refs[:n], refs[n:2 * n]
        send_sems, recv_sems = refs[2 * n:]
        x, y, c, _ = _place()
        copies = [pltpu.make_async_remote_copy(src_ref=g_refs[i].at[:, _half(c, g_refs[i].shape[1], 16)[1]], dst_ref=o_refs[i],
                                               send_sem=send_sems.at[i], recv_sem=recv_sems.at[i], device_id=(x, y, 1 - c), device_id_type=MESH)
                  for i in range(n)]
        for cp in copies:
            cp.start()
        for cp in copies:
            cp.wait()

    return pl.pallas_call(
        body, name="grad_to_sibling", out_shape=[SDS((4, t.shape[1] // 2, t.shape[2]), t.dtype) for t in grads],
        in_specs=[_HBM] * n, out_specs=[_HBM] * n, scratch_shapes=[pltpu.SemaphoreType.DMA((n,)), pltpu.SemaphoreType.DMA((n,))],
    )(*grads)


def _scatter_to_chips(parts):
    n = len(parts)

    def body(*refs):
        p_refs, o_refs = refs[:n], refs[n:2 * n]
        send_sems, recv_sems = refs[2 * n:]
        x, y, c, chips = _place()
        copies = [pltpu.make_async_remote_copy(src_ref=p_refs[i].at[2 * chip[0] + chip[1]], dst_ref=o_refs[i].at[j], send_sem=send_sems.at[3 * i + j],
                                               recv_sem=recv_sems.at[3 * i + j], device_id=(*chip, c), device_id_type=MESH)
                  for j, chip in enumerate(chips) for i in range(n)]
        for cp in copies:
            cp.start()
        for cp in copies:
            cp.wait()

    return pl.pallas_call(
        body, name="grad_to_chips", out_shape=[SDS((3,) + t.shape[1:], t.dtype) for t in parts], in_specs=[_HBM] * n, out_specs=[_HBM] * n,
        scratch_shapes=[pltpu.SemaphoreType.DMA((3 * n,)), pltpu.SemaphoreType.DMA((3 * n,))],
    )(*parts)


def _join_halves(bufs):
    n = len(bufs)

    def body(*refs):
        o_refs = refs[n:2 * n]
        send_sems, recv_sems = refs[2 * n:]
        x, y, c, _ = _place()
        sends, recvs = [], []
        for i in range(n):
            mine, theirs = _half(c, o_refs[i].shape[0], 8)
            sends.append(pltpu.make_async_remote_copy(src_ref=o_refs[i].at[mine], dst_ref=o_refs[i].at[mine], send_sem=send_sems.at[i],
                                                      recv_sem=recv_sems.at[i], device_id=(x, y, 1 - c), device_id_type=MESH))
            recvs.append(pltpu.make_async_remote_copy(src_ref=o_refs[i].at[mine], dst_ref=o_refs[i].at[theirs], send_sem=send_sems.at[i],
                                                      recv_sem=recv_sems.at[i], device_id=(x, y, 1 - c), device_id_type=MESH))
        for cp in sends:
            cp.start()
        for cp in recvs:
            cp.wait_recv()
        for cp in sends:
            cp.wait_send()

    return pl.pallas_call(
        body, name="grad_join_halves", out_shape=[SDS(t.shape, t.dtype) for t in bufs], in_specs=[_HBM] * n, out_specs=[_HBM] * n,
        input_output_aliases={i: i for i in range(n)},
        scratch_shapes=[pltpu.SemaphoreType.DMA((n,)), pltpu.SemaphoreType.DMA((n,))],
    )(*bufs)


def _add_sibling(name, grads, from_sib, place, n_tiles):
    n = len(grads)

    def body(place_ref, *refs):
        for i in range(n):
            refs[2 * n + i][...] = (refs[i][...].astype(F32) + refs[n + i][...].astype(F32)).astype(refs[2 * n + i].dtype)

    blocks = [(4, t.shape[1] // n_tiles, t.shape[2]) for t in from_sib]
    return pl.pallas_call(
        body, name=name,
        grid_spec=pltpu.PrefetchScalarGridSpec(
            num_scalar_prefetch=1, grid=(n_tiles,),
            in_specs=[pl.BlockSpec(b, lambda t, p: (0, p[0] * n_tiles + t, 0)) for b in blocks] + [pl.BlockSpec(b, lambda t, p: (0, t, 0)) for b in blocks],
            out_specs=[pl.BlockSpec(b, lambda t, p: (0, t, 0)) for b in blocks]),
        out_shape=[SDS(t.shape, t.dtype) for t in from_sib], compiler_params=_params(("parallel",)),
    )(place, *grads, *from_sib)


def _add_chips(name, parts, from_chips, place, n_tiles):
    n = len(parts)

    def body(place_ref, *refs):
        for i in range(n):
            p_ref, r_ref = refs[i], refs[n + i]
            refs[2 * n + i][...] = ((p_ref[0].astype(F32) + r_ref[0].astype(F32)) + r_ref[1].astype(F32)) + r_ref[2].astype(F32)

    tiles = [(t.shape[1] // n_tiles, t.shape[2]) for t in parts]
    return pl.pallas_call(
        body, name=name,
        grid_spec=pltpu.PrefetchScalarGridSpec(
            num_scalar_prefetch=1, grid=(n_tiles,),
            in_specs=[pl.BlockSpec((1,) + b, lambda t, p: (p[1], t, 0)) for b in tiles] + [pl.BlockSpec((3,) + b, lambda t, p: (0, t, 0)) for b in tiles],
            out_specs=[pl.BlockSpec(b, lambda t, p: (p[0] * n_tiles + t, 0)) for b in tiles]),
        out_shape=[SDS((2 * t.shape[1], t.shape[2]), F32) for t in parts], compiler_params=_params(("parallel",)),
    )(place, *parts, *from_chips)


def _reduce_scatter(grads):
    place = jnp.stack([lax.axis_index("c"), 2 * lax.axis_index("x") + lax.axis_index("y")]).astype(jnp.int32)
    from_sib = _swap_with_sibling(grads)
    part = _add_sibling("grad_add_sibling_wide", grads[:1], from_sib[:1], place, 4) + _add_sibling("grad_add_sibling", grads[1:], from_sib[1:], place, 2)
    from_chips = _scatter_to_chips(part)
    halves = _add_chips("grad_add_chips_wide", part[:1], from_chips[:1], place, 4) + _add_chips("grad_add_chips", part[1:], from_chips[1:], place, 2)
    return _join_halves(halves)


def _all_reduce_small(x):
    _, n, wd = x.shape

    def body(x_ref, o_ref, buf, send1, recv1, send2, recv2):
        mx, my, mc = lax.axis_index("x"), lax.axis_index("y"), lax.axis_index("c")
        me = 4 * mx + 2 * my + mc
        peers = []
        for rel in range(1, 8):
            px, py, pc = (1 - mx if rel & 4 else mx), (1 - my if rel & 2 else my), (1 - mc if rel & 1 else mc)
            peers.append(((px, py, pc), 4 * px + 2 * py + pc))
        out = [pltpu.make_async_remote_copy(src_ref=x_ref.at[pidx], dst_ref=buf.at[j], send_sem=send1.at[j], recv_sem=recv1.at[j],
                                            device_id=peer, device_id_type=MESH) for j, (peer, pidx) in enumerate(peers)]
        for cp in out:
            cp.start()
        for cp in out:
            cp.wait()
        acc = x_ref[me]
        for j in range(7):
            acc = acc + buf[j]
        o_ref[me] = acc
        back = [pltpu.make_async_remote_copy(src_ref=o_ref.at[me], dst_ref=o_ref.at[me], send_sem=send2.at[j], recv_sem=recv2.at[j],
                                             device_id=peer, device_id_type=MESH) for j, (peer, _) in enumerate(peers)]
        for cp in back:
            cp.start()
        for j, (peer, pidx) in enumerate(peers):
            pltpu.make_async_remote_copy(src_ref=o_ref.at[me], dst_ref=o_ref.at[pidx], send_sem=send2.at[j], recv_sem=recv2.at[j],
                                         device_id=peer, device_id_type=MESH).wait_recv()
        for cp in back:
            cp.wait_send()

    vm = pl.BlockSpec(memory_space=pltpu.VMEM)
    return pl.pallas_call(
        body, name="all_reduce_small", out_shape=SDS(x.shape, F32), in_specs=[vm], out_specs=vm,
        scratch_shapes=[pltpu.VMEM((7, n, wd), F32)] + [pltpu.SemaphoreType.DMA((7,))] * 4,
        compiler_params=pltpu.CompilerParams(vmem_limit_bytes=VMEM_LIMIT_BYTES),
    )(x)


def _rows128(t):
    if t.size % 128 == 0:
        return t.reshape(-1, 128)
    assert t.size < 128
    return jnp.pad(t.reshape(1, -1), ((0, 0), (0, 128 - t.size)))


def _all_reduce_arrays(arrays):
    rows = [_rows128(t) for t in arrays]
    total = sum(r.shape[0] for r in rows)
    padded = -(-total // 64) * 64
    buf = jnp.concatenate(rows + [jnp.zeros((padded - total, 128), F32)], axis=0).reshape(8, padded // 8, 128)
    summed = _all_reduce_small(buf).reshape(padded, 128)
    out, off = [], 0
    for t, r in zip(arrays, rows):
        blk = summed[off:off + r.shape[0]]
        out.append(blk.reshape(t.shape) if t.size % 128 == 0 else blk[0, :t.size].reshape(t.shape))
        off += r.shape[0]
    return out


def _column_slices(sources, lo, hi):
    out = []
    for arr, a, b in sources:
        s, e = max(lo, a), min(hi, b)
        if s < e:
            out.append(arr[:, s - a:e - a])
    return out


def _adamw_math(w, g, m, v):
    m = ADAM_B1 * m + (1.0 - ADAM_B1) * g
    v = ADAM_B2 * v + (1.0 - ADAM_B2) * jnp.square(g)
    m_hat = m / (1.0 - ADAM_B1 ** ADAM_STEP)
    v_hat = v / (1.0 - ADAM_B2 ** ADAM_STEP)
    delta = -ADAM_LR * (m_hat / (jnp.sqrt(v_hat) + ADAM_EPS) + ADAM_WD * w)
    return delta, m, v


def _adamw_large(name, w, g, m, v):
    shape = w.shape
    cols = shape[-1]
    rows = int(np.prod(shape[:-1]))
    tile = 256 if rows % 256 == 0 else rows
    args = [t.reshape(rows, cols) for t in (w, g, m, v)]
    outs = _rowwise(name, _adamw_math, [(a, cols, 0) for a in args], [], [((), cols, F32)] * 3, tile=tile)
    return [o.reshape(shape) for o in outs]


def _adamw_small(ws, gs, ms, vs):
    n = len(ws)

    def body(*refs):
        for i in range(n):
            d, m, v = _adamw_math(refs[i][...], refs[n + i][...], refs[2 * n + i][...], refs[3 * n + i][...])
            refs[4 * n + i][...] = d
            refs[5 * n + i][...] = m
            refs[6 * n + i][...] = v

    outs = pl.pallas_call(body, name="adamw_small", out_shape=[SDS(w.shape, F32) for w in ws] * 3,
                          compiler_params=pltpu.CompilerParams(vmem_limit_bytes=VMEM_LIMIT_BYTES))(*ws, *gs, *ms, *vs)
    return outs[:n], outs[n:2 * n], outs[2 * n:]


def _rotary_tables(s):
    half = HEAD_DIM // 2
    inv = ROPE_BASE ** (-jnp.arange(half, dtype=F32) / half)
    ang = jnp.arange(s, dtype=F32)[:, None] * inv[None, :]
    cos, sin = jnp.cos(ang), jnp.sin(ang)
    return jnp.concatenate([cos, cos], axis=-1), jnp.concatenate([-sin, sin], axis=-1)


def kernel(x, mix_pre_norm, mix_post_norm, w_in, dn_conv, dn_A_log, dn_dt_bias, dn_out_norm, ret_out_norm, pool_w, pool_scale, w_branch_dn, w_branch_ret, w_branch_pool, w_out, ffn_pre_norm, ffn_post_norm, ffn_gate, ffn_up, ffn_down, loss_target, m_mix_pre_norm, m_mix_post_norm, m_w_in, m_dn_conv, m_dn_A_log, m_dn_dt_bias, m_dn_out_norm, m_ret_out_norm, m_pool_w, m_pool_scale, m_w_branch_dn, m_w_branch_ret, m_w_branch_pool, m_w_out, m_ffn_pre_norm, m_ffn_post_norm, m_ffn_gate, m_ffn_up, m_ffn_down, v_mix_pre_norm, v_mix_post_norm, v_w_in, v_dn_conv, v_dn_A_log, v_dn_dt_bias, v_dn_out_norm, v_ret_out_norm, v_pool_w, v_pool_scale, v_w_branch_dn, v_w_branch_ret, v_w_branch_pool, v_w_out, v_ffn_pre_norm, v_ffn_post_norm, v_ffn_gate, v_ffn_up, v_ffn_down):
    depth = w_in.shape[0]
    s, d = x.shape[1], x.shape[2]
    d_in4 = w_in.shape[2]
    d_in = 4 * d_in4
    f4 = ffn_gate.shape[2]
    qkv_w = 3 * BRANCH_W
    g0 = 3 * d
    b_qkv, b_z = g0 // HEAD_DIM, (g0 + qkv_w) // HEAD_DIM
    b_rq, b_rk, b_rv, b_rg, b_pu = ((g0 + qkv_w + BRANCH_W * i) // HEAD_DIM for i in range(1, 6))
    main_w = g0 + qkv_w + 6 * BRANCH_W
    src_tail = qkv_w + BRANCH_W
    assert d_in == main_w + 2 * HEADS
    col_tile = 1536 if main_w % 1536 == 0 else 128
    tile = 256
    xk = lax.axis_index("x") * 2 + lax.axis_index("y")

    cosf, sinf = _rotary_tables(s)
    lgam = jnp.zeros((1, HEAD_DIM), F32).at[0, :HEADS].set(jnp.log1p(-jnp.exp2(-5.0 - jnp.arange(HEADS, dtype=F32))))

    def pad_lanes(vec):
        return jnp.zeros((1, HEAD_DIM), F32).at[0, :vec.shape[0]].set(vec)

    gathered = [_gather_weights([t[l].astype(WIRE_DTYPE) for t in (w_in, w_branch_dn, w_branch_ret, w_branch_pool, w_out, ffn_gate, ffn_up, ffn_down)])
                for l in range(depth)]
    conv_place = lax.dynamic_update_slice(jnp.zeros((4,) + dn_conv.shape, F32), dn_conv[None], (xk, 0, 0, 0))
    conv_sum = _all_reduce_arrays([conv_place * (1 - lax.axis_index("c")).astype(F32)])[0]
    conv_all = jnp.concatenate([conv_sum[k] for k in range(4)], axis=-1)
    rest = src_tail + 2 * HEADS

    def layer_weights(l):
        wi, wbd, wbr, wbp, wo, wg, wu, wdn = gathered[l]
        pieces = [(wi[k], k * d_in4, (k + 1) * d_in4) for k in range(4)]
        w_main = jnp.concatenate(_column_slices(pieces, rest + 5 * BRANCH_W, d_in) + _column_slices(pieces, 0, src_tail)
                                 + _column_slices(pieces, rest, rest + 5 * BRANCH_W), axis=1)
        w_tail = jnp.concatenate(_column_slices(pieces, src_tail, rest) + [jnp.zeros((d, TAIL_W - 2 * HEADS), wi.dtype)], axis=1)
        wb = [jnp.concatenate([t[k] for k in range(4)], axis=1) for t in (wbd, wbp, wbr)]
        return dict(main=w_main, tail=w_tail, wb=wb, out=wo.reshape(d, d), gate=wg, up=wu, down=wdn)

    def norm_fwd(name, xin, w):
        return _rowwise(name, lambda xv, wv: (_rms(xv, wv),), [(xin, d, 0)], [w], [((), d, MXU_DTYPE)], tile=tile)[0]

    def resnorm_fwd(name, xin, t, w):
        return _rowwise(name, lambda xv, tv, wv: (xv + _rms(tv, wv),), [(xin, d, 0), (t, d, 0)], [w], [((), d, F32)], tile=tile)[0]

    def merge_math(gates, p_dn, p_pool, p_ret):
        gt = jax.nn.sigmoid(gates)
        return gt[:, :d] * p_dn + gt[:, d:2 * d] * p_pool + gt[:, 2 * d:] * p_ret

    def swiglu_math(u, v):
        return _silu(u) * v

    xs, saved = [x[0]], []
    for l in range(depth):
        wts = layer_weights(l)
        xin = xs[-1]
        h = norm_fwd("mix_pre_norm_fwd", xin, mix_pre_norm[l][None])
        projm = _mm("proj_main", h, wts["main"], tn=col_tile)
        projt = _mm("proj_tail", h, wts["tail"])
        convw = conv_all[l]
        qkv_c = _conv_fwd(projm, convw, b_qkv)
        dn_par = [pad_lanes(dn_A_log[l]), pad_lanes(dn_dt_bias[l]), dn_out_norm[l][None]]
        dn_rows = [(qkv_c, 0, True), (qkv_c, HEADS, True), (qkv_c, 2 * HEADS, True), (projm, b_z, True), (projt, 0, False)]
        o_dn, dn_states = _scan_fwd("dn_fwd", _dn_group, dn_rows, dn_par, s)
        ret_par = [ret_out_norm[l].reshape(HEADS, HEAD_DIM), lgam]
        ret_rows = [(projm, b_rq, True), (projm, b_rk, True), (projm, b_rv, True), (projm, b_rg, True), (cosf, 0, False), (sinf, 0, False)]
        o_ret, ret_states = _scan_fwd("ret_fwd", _ret_group, ret_rows, ret_par, s)
        o_pool = _pool_fwd(projm, pool_w[l], pool_scale[l][None], b_pu)
        p_dn = _mm("branch_dn", o_dn, wts["wb"][0])
        p_pool = _mm("branch_pool", o_pool, wts["wb"][1])
        p_ret = _mm("branch_ret", o_ret, wts["wb"][2])
        ymix = _rowwise("merge_fwd", lambda *a: (merge_math(*a),), [(projm, 3 * d, 0), (p_dn, d, 0), (p_pool, d, 0), (p_ret, d, 0)], [],
                        [((), d, MXU_DTYPE)], tile=tile)[0]
        mo = _mm("mix_out", ymix, wts["out"])
        xmid = resnorm_fwd("mix_post_norm_fwd", xin, mo, mix_post_norm[l][None])
        h2 = norm_fwd("ffn_pre_norm_fwd", xmid, ffn_pre_norm[l][None])
        fu = _mm("ffn_gate", h2, wts["gate"])
        fv = _mm("ffn_up", h2, wts["up"])
        act = _rowwise("swiglu_fwd", lambda u, v: (swiglu_math(u, v),), [(fu, f4, 0), (fv, f4, 0)], [], [((4,), f4, MXU_DTYPE)], tile=tile)[0]
        fo = _mm("ffn_down", act, wts["down"], contract_batch=True)
        xout = resnorm_fwd("ffn_post_norm_fwd", xmid, fo, ffn_post_norm[l][None])
        xs.append(xout)
        saved.append(dict(wts=wts, h=h, projm=projm, convw=convw, dn_rows=dn_rows, dn_par=dn_par, dn_states=dn_states,
                          ret_rows=ret_rows, ret_par=ret_par, ret_states=ret_states, o_dn=o_dn, o_pool=o_pool, o_ret=o_ret,
                          p=(p_dn, p_pool, p_ret), ymix=ymix, mo=mo, xmid=xmid, h2=h2, fu=fu, fv=fv, act=act, fo=fo, xin=xin))

    def loss_math(y, t):
        diff = y - t
        part = 0.5 * jnp.sum(jnp.sum(diff * diff, axis=-1, keepdims=True) / d, axis=0, keepdims=True)
        return diff / d, jnp.broadcast_to(part, (8, 128))

    gx, loss_part = _rowwise("loss_head", loss_math, [(xs[-1], d, 0), (loss_target[0], d, 0)], [], [((), d, F32)], [((8, 128), F32)], tile=tile)
    loss = lax.psum(loss_part[0, 0], ("x", "y", "c"))

    def resnorm_bwd(name, t, gout, w):
        def fn(tv, gv, wv):
            _, vjp = jax.vjp(_rms, tv, wv)
            return vjp(gv)
        return _rowwise(name, fn, [(t, d, 0), (gout, d, 0)], [w], [((), d, F32)], [((1, d), F32)], tile=tile)

    def norm_bwd(name, xin, dh_a, dh_b, gout, w):
        def fn(xv, da, db, gv, wv):
            _, vjp = jax.vjp(_rms, xv, wv)
            dx_, dw_ = vjp(da + db)
            return gv + dx_, dw_
        return _rowwise(name, fn, [(xin, d, 0), (dh_a, d, 0), (dh_b, d, 0), (gout, d, 0)], [w], [((), d, F32)], [((1, d), F32)], tile=tile)

    def swiglu_bwd(u, v, da):
        _, vjp = jax.vjp(swiglu_math, u, v)
        return vjp(da)

    def merge_bwd(gates, p_dn, p_pool, p_ret, dyv):
        _, vjp = jax.vjp(merge_math, gates, p_dn, p_pool, p_ret)
        return vjp(dyv)

    big_grads, small_grads = [None] * depth, [None] * depth
    for l in reversed(range(depth)):
        sv = saved[l]
        wts = sv["wts"]
        dfo, d_fpost = resnorm_bwd("ffn_post_norm_bwd", sv["fo"], gx, ffn_post_norm[l][None])
        dact = _mm("ffn_down_dx", dfo, wts["down"], tb=True)
        g_down = _mm("ffn_down_dw", sv["act"], dfo, ta=True, out_dtype=WIRE_DTYPE, tm=f4)
        du, dv = _rowwise("swiglu_bwd", swiglu_bwd, [(sv["fu"], f4, 0), (sv["fv"], f4, 0), (dact, f4, 0)], [],
                          [((4,), f4, MXU_DTYPE), ((4,), f4, MXU_DTYPE)], tile=tile)
        g_gate = _mm("ffn_gate_dw", sv["h2"], du, ta=True, out_dtype=WIRE_DTYPE)
        g_up = _mm("ffn_up_dw", sv["h2"], dv, ta=True, out_dtype=WIRE_DTYPE)
        dh2_a = _mm("ffn_gate_dx", du, wts["gate"], tb=True, contract_batch=True)
        dh2_b = _mm("ffn_up_dx", dv, wts["up"], tb=True, contract_batch=True)
        gmid, d_fpre = norm_bwd("ffn_pre_norm_bwd", sv["xmid"], dh2_a, dh2_b, gx, ffn_pre_norm[l][None])
        dmo, d_post = resnorm_bwd("mix_post_norm_bwd", sv["mo"], gmid, mix_post_norm[l][None])
        dy = _mm("mix_out_dx", dmo, wts["out"], tb=True)
        g_out = _mm("mix_out_dw", sv["ymix"], dmo, ta=True, out_dtype=WIRE_DTYPE)
        dgates, dp_dn, dp_pool, dp_ret = _rowwise(
            "merge_bwd", merge_bwd, [(sv["projm"], 3 * d, 0), (sv["p"][0], d, 0), (sv["p"][1], d, 0), (sv["p"][2], d, 0), (dy, d, 0)], [],
            [((), 3 * d, MXU_DTYPE), ((), d, MXU_DTYPE), ((), d, MXU_DTYPE), ((), d, MXU_DTYPE)], tile=tile)
        do_dn = _mm("branch_dn_dx", dp_dn, wts["wb"][0], tb=True)
        do_pool = _mm("branch_pool_dx", dp_pool, wts["wb"][1], tb=True)
        do_ret = _mm("branch_ret_dx", dp_ret, wts["wb"][2], tb=True)
        g_wb_dn = _mm("branch_dn_dw", sv["o_dn"], dp_dn, ta=True, out_dtype=WIRE_DTYPE, tm=BRANCH_W)
        g_wb_pool = _mm("branch_pool_dw", sv["o_pool"], dp_pool, ta=True, out_dtype=WIRE_DTYPE, tm=BRANCH_W)
        g_wb_ret = _mm("branch_ret_dw", sv["o_ret"], dp_ret, ta=True, out_dtype=WIRE_DTYPE, tm=BRANCH_W)
        d_rq, d_rk, d_rv, d_rg, d_retw, _ = _scan_bwd("ret_bwd", _ret_group, sv["ret_rows"], 4, sv["ret_par"], sv["ret_states"], do_ret, s)
        d_pu, d_poolw, d_pools = _pool_bwd(sv["projm"], pool_w[l], pool_scale[l][None], do_pool, b_pu)
        d_qc, d_kc, d_vc, d_z, d_tail, d_alog, d_dtb, d_dnw = _scan_bwd("dn_bwd", _dn_group, sv["dn_rows"], 5, sv["dn_par"], sv["dn_states"], do_dn, s)
        d_qkv, d_convw = _conv_bwd(sv["projm"], sv["convw"], jnp.concatenate([d_qc, d_kc, d_vc], axis=1), b_qkv)
        dprojm = jnp.concatenate([dgates] + [t.astype(MXU_DTYPE) for t in (d_qkv, d_z, d_rq, d_rk, d_rv, d_rg, d_pu)], axis=1)
        dh_a = _mm("proj_main_dx", dprojm, wts["main"], tb=True, tk=col_tile)
        dh_b = _mm("proj_tail_dx", d_tail, wts["tail"], tb=True)
        g_main = _mm("proj_main_dw", sv["h"], dprojm, ta=True, out_dtype=WIRE_DTYPE, tn=col_tile)
        g_tail = _mm("proj_tail_dw", sv["h"], d_tail, ta=True, out_dtype=WIRE_DTYPE)
        gx, d_pre = norm_bwd("mix_pre_norm_bwd", sv["xin"], dh_a, dh_b, gmid, mix_pre_norm[l][None])
        canon = [(g_main[:, g0:g0 + src_tail], 0, src_tail), (g_tail[:, :2 * HEADS], src_tail, rest),
                 (g_main[:, g0 + src_tail:], rest, rest + 5 * BRANCH_W), (g_main[:, :g0], rest + 5 * BRANCH_W, d_in)]
        g_in = jnp.stack([jnp.concatenate(_column_slices(canon, k * d_in4, (k + 1) * d_in4), axis=1) for k in range(4)])
        split_cols = lambda t: jnp.stack([t[:, k * (d // 4):(k + 1) * (d // 4)] for k in range(4)])
        big_grads[l] = [g_in, split_cols(g_wb_dn), split_cols(g_wb_ret), split_cols(g_wb_pool), g_out.reshape(4, d // 4, d), g_gate, g_up, g_down]
        small_grads[l] = [d_pre[0], d_post[0], d_convw, d_alog[0, :HEADS], d_dtb[0, :HEADS], d_dnw[0], d_retw.reshape(-1), d_poolw, d_pools[0],
                          d_fpre[0], d_fpost[0]]

    reduced = [_reduce_scatter(big_grads[l]) for l in range(depth)]
    big_g = [jnp.stack([reduced[l][i] for l in range(depth)]) for i in range(8)]
    small_stacked = [jnp.stack([small_grads[l][i] for l in range(depth)]) for i in range(len(small_grads[0]))]
    (g_pre, g_post, g_conv_full, g_alog, g_dtb, g_dnw, g_retw, g_poolw, g_pools, g_fpre, g_fpost) = _all_reduce_arrays(small_stacked)
    g_conv = lax.dynamic_slice_in_dim(g_conv_full.reshape(depth, 4, 4, qkv_w // 4), xk, 1, axis=2)[:, :, 0, :]

    g_w_in, g_wbd, g_wbr, g_wbp, g_wo, g_fg, g_fu, g_fd = big_g
    large = [("w_in", w_in, g_w_in, m_w_in, v_w_in), ("w_branch_dn", w_branch_dn, g_wbd, m_w_branch_dn, v_w_branch_dn),
             ("w_branch_ret", w_branch_ret, g_wbr, m_w_branch_ret, v_w_branch_ret), ("w_branch_pool", w_branch_pool, g_wbp, m_w_branch_pool, v_w_branch_pool),
             ("w_out", w_out, g_wo, m_w_out, v_w_out), ("ffn_gate", ffn_gate, g_fg, m_ffn_gate, v_ffn_gate),
             ("ffn_up", ffn_up, g_fu, m_ffn_up, v_ffn_up), ("ffn_down", ffn_down, g_fd, m_ffn_down, v_ffn_down)]
    small = [("mix_pre_norm", mix_pre_norm, g_pre, m_mix_pre_norm, v_mix_pre_norm), ("mix_post_norm", mix_post_norm, g_post, m_mix_post_norm, v_mix_post_norm),
             ("dn_conv", dn_conv, g_conv, m_dn_conv, v_dn_conv), ("dn_A_log", dn_A_log, g_alog, m_dn_A_log, v_dn_A_log),
             ("dn_dt_bias", dn_dt_bias, g_dtb, m_dn_dt_bias, v_dn_dt_bias), ("dn_out_norm", dn_out_norm, g_dnw, m_dn_out_norm, v_dn_out_norm),
             ("ret_out_norm", ret_out_norm, g_retw, m_ret_out_norm, v_ret_out_norm), ("pool_w", pool_w, g_poolw, m_pool_w, v_pool_w),
             ("pool_scale", pool_scale, g_pools, m_pool_scale, v_pool_scale), ("ffn_pre_norm", ffn_pre_norm, g_fpre, m_ffn_pre_norm, v_ffn_pre_norm),
             ("ffn_post_norm", ffn_post_norm, g_fpost, m_ffn_post_norm, v_ffn_post_norm)]
    flat2 = lambda t: t.reshape(-1, t.shape[-1])
    sd, sm, sv_ = _adamw_small(*[[flat2(t[i]) for t in small] for i in (1, 2, 3, 4)])
    grads, deltas, new_m, new_v = {}, {}, {}, {}
    for name, w, g, m, v in large:
        grads[name] = g
        deltas[name], new_m[name], new_v[name] = _adamw_large("adamw_" + name, w, g, m, v)
    for i, (name, w, g, m, v) in enumerate(small):
        grads[name] = g.reshape(w.shape)
        deltas[name], new_m[name], new_v[name] = sd[i].reshape(w.shape), sm[i].reshape(w.shape), sv_[i].reshape(w.shape)
    order = ["mix_pre_norm", "mix_post_norm", "w_in", "dn_conv", "dn_A_log", "dn_dt_bias", "dn_out_norm", "ret_out_norm", "pool_w", "pool_scale",
             "w_branch_dn", "w_branch_ret", "w_branch_pool", "w_out", "ffn_pre_norm", "ffn_post_norm", "ffn_gate", "ffn_up", "ffn_down"]
    return (loss, gx[None], *[grads[n] for n in order], *[deltas[n] for n in order], *[new_m[n] for n in order], *[new_v[n] for n in order])
```
